```python
import math
import jax, jax.numpy as jnp
from jax import lax
import numpy as np

D_MODEL = 1024
BATCH = 8
SEQ = 2048
DEPTH = 1
DEC_BATCH = 32
DEC_SEQ = 4
PAST_LEN = 16384
PAGE_SIZE = 128

MLA_HEADS = 8
Q_LORA = 384
KV_LORA = 256
QK_NOPE = 128
QK_ROPE = 64
V_HEAD = D_MODEL // MLA_HEADS
MLA_SCALE = (QK_NOPE + QK_ROPE) ** -0.5
RET_HEADS = 8
RET_DK = 128
RET_DV = D_MODEL // RET_HEADS
RET_CHUNK = 128
D_FF = 4 * D_MODEL
ROPE_BASE = 10000.0
LN_EPS = 1e-5
RMS_EPS = 1e-6
Q_BLOCK = 128
NEG = -1e30
ALPHA = (2.0 * DEPTH) ** 0.25
BETA = (8.0 * DEPTH) ** -0.25
IN_SIZES = (Q_LORA, KV_LORA, QK_ROPE, RET_HEADS * RET_DK, RET_HEADS * RET_DK,
            RET_HEADS * RET_DV, RET_HEADS * RET_DV, D_MODEL, D_MODEL)

kernel_name = "hybrid_mla_retention_deepnorm_adaln_step"


def layer_norm(x, g, b):
    xf = x.astype(jnp.float32)
    mu = xf.mean(-1, keepdims=True)
    var = jnp.square(xf - mu).mean(-1, keepdims=True)
    return ((xf - mu) * lax.rsqrt(var + LN_EPS) * g + b).astype(x.dtype)


def rms_norm(x, g):
    xf = x.astype(jnp.float32)
    return (xf * lax.rsqrt(jnp.square(xf).mean(-1, keepdims=True) + RMS_EPS) * g).astype(x.dtype)


def rope(x, pos):
    half = x.shape[-1] // 2
    inv = ROPE_BASE ** (-jnp.arange(half, dtype=jnp.float32) / half)
    ang = pos.astype(jnp.float32)[:, None] * inv[None, :]
    ang = ang.reshape(ang.shape[:1] + (1,) * (x.ndim - 3) + (half,))
    cos, sin = jnp.cos(ang), jnp.sin(ang)
    x1 = x[..., :half].astype(jnp.float32)
    x2 = x[..., half:].astype(jnp.float32)
    return jnp.concatenate([x1 * cos - x2 * sin, x1 * sin + x2 * cos], -1).astype(x.dtype)


def retention_chunks(q, k, v, s0, chunk):
    B, T, H, DK = q.shape
    nc = T // chunk
    lg = jnp.log(1.0 - 2.0 ** (-5.0 - jnp.arange(H, dtype=jnp.float32)))
    idx = jnp.arange(chunk, dtype=jnp.float32)
    diff = idx[:, None] - idx[None, :]
    causal = diff >= 0
    dmask = jnp.exp(jnp.where(causal, diff, 0.0)[None] * lg[:, None, None]) * causal[None]
    q_decay = jnp.exp((idx[:, None] + 1.0) * lg[None, :])[None, :, :, None]
    k_decay = jnp.exp((chunk - 1.0 - idx[:, None]) * lg[None, :])[None, :, :, None]
    chunk_decay = jnp.exp(chunk * lg)[None, :, None, None]

    def to_chunks(a):
        return a.astype(jnp.float32).reshape(B, nc, chunk, H, a.shape[-1]).swapaxes(0, 1)

    def step(s, inp):
        qc, kc, vc = inp
        scores = jnp.einsum('bihd,bjhd->bhij', qc, kc) * dmask[None]
        inner = jnp.einsum('bhij,bjhe->bihe', scores, vc)
        cross = jnp.einsum('bihd,bhde->bihe', qc, s) * q_decay
        s_new = chunk_decay * s + jnp.einsum('bjhd,bjhe->bhde', kc * k_decay, vc)
        return s_new, inner + cross

    s_fin, out = lax.scan(step, s0.astype(jnp.float32), (to_chunks(q), to_chunks(k), to_chunks(v)))
    out = out.swapaxes(0, 1).reshape(B, T, H, v.shape[-1])
    return out, s_fin.astype(s0.dtype)


def mla_scores(q_lat, q_pe, c_kv, k_pe):
    s = jnp.einsum('bthr,blr->bhtl', q_lat, c_kv, preferred_element_type=jnp.float32)
    s = s + jnp.einsum('bthd,bld->bhtl', q_pe, k_pe, preferred_element_type=jnp.float32)
    return s * MLA_SCALE


def attend_prompt(q_lat, q_pe, c_kv, k_pe):
    B, S, H, R = q_lat.shape
    qb = min(Q_BLOCK, S)
    kpos = jnp.arange(S)

    def block(i):
        ql = lax.dynamic_slice_in_dim(q_lat, i * qb, qb, axis=1)
        qp = lax.dynamic_slice_in_dim(q_pe, i * qb, qb, axis=1)
        qpos = i * qb + jnp.arange(qb)
        s = jnp.where(kpos[None, :] <= qpos[:, None], mla_scores(ql, qp, c_kv, k_pe), NEG)
        p = jax.nn.softmax(s, axis=-1)
        return jnp.einsum('bhtl,blr->bthr', p.astype(c_kv.dtype), c_kv)

    out = lax.map(block, jnp.arange(S // qb))
    return out.swapaxes(0, 1).reshape(B, S, H, R)


def make_attend_sample(cache_ckv, cache_kpe, page_table):
    DB, n_pages = page_table.shape
    past_len = n_pages * PAGE_SIZE

    def attend(q_lat, q_pe, c_kv, k_pe):
        T = q_lat.shape[1]
        c_past = cache_ckv[page_table].reshape(DB, past_len, cache_ckv.shape[-1])
        kr_past = cache_kpe[page_table].reshape(DB, past_len, cache_kpe.shape[-1])
        t = jnp.arange(T)
        s_new = jnp.where(t[None, :] <= t[:, None], mla_scores(q_lat, q_pe, c_kv, k_pe), NEG)
        s = jnp.concatenate([mla_scores(q_lat, q_pe, c_past, kr_past), s_new], axis=-1)
        p = jax.nn.softmax(s, axis=-1)
        out = jnp.einsum('bhtl,blr->bthr', p[..., :past_len].astype(c_past.dtype), c_past)
        return out + jnp.einsum('bhtl,blr->bthr', p[..., past_len:].astype(c_kv.dtype), c_kv)

    return attend, past_len


def decoder_layer(x, c, pos, attend, ret_s0, ret_chunk, w):
    B, T, _ = x.shape
    ada = (jax.nn.silu(c) @ w['w_ada'] + w['b_ada'])[:, None, :]
    sh1, sc1, g1, sh2, sc2, g2 = jnp.split(ada, 6, axis=-1)

    h = x * (1.0 + sc1) + sh1
    splits = np.cumsum(IN_SIZES)[:-1].tolist()
    p_q, p_kv, p_kr, r_q, r_k, r_v, r_g, gate_a, gate_r = jnp.split(h @ w['w_in'], splits, axis=-1)

    q = jnp.einsum('btr,rhd->bthd', rms_norm(p_q, w['g_qnorm']), w['w_uq'])
    q_nope, q_pe = q[..., :QK_NOPE], rope(q[..., QK_NOPE:], pos)
    c_kv = rms_norm(p_kv, w['g_kvnorm'])
    k_pe = rope(p_kr, pos)
    q_lat = jnp.einsum('bthd,rhd->bthr', q_nope, w['w_uk'])
    out_lat = attend(q_lat, q_pe, c_kv, k_pe)
    o_a = jnp.einsum('bthr,rhe->bthe', out_lat, w['w_uv']).reshape(B, T, MLA_HEADS * V_HEAD)

    rq = rope(r_q.reshape(B, T, RET_HEADS, RET_DK), pos)
    rk = rope(r_k.reshape(B, T, RET_HEADS, RET_DK), pos) * (RET_DK ** -0.5)
    rv = r_v.reshape(B, T, RET_HEADS, RET_DV)
    y, s_new = retention_chunks(rq, rk, rv, ret_s0, ret_chunk)
    mu = y.mean(-1, keepdims=True)
    y = (y - mu) * lax.rsqrt(jnp.square(y - mu).mean(-1, keepdims=True) + LN_EPS)
    o_r = jax.nn.silu(r_g) * y.reshape(B, T, RET_HEADS * RET_DV).astype(x.dtype)

    mix = (jax.nn.sigmoid(gate_a) * o_a + jax.nn.sigmoid(gate_r) * o_r) @ w['w_o']
    x = layer_norm(ALPHA * x + (1.0 + g1) * mix, w['ln1_g'], w['ln1_b'])

    h = x * (1.0 + sc2) + sh2
    m = jnp.square(jax.nn.relu(h @ w['w_up'])) @ w['w_down']
    x = layer_norm(ALPHA * x + (1.0 + g2) * m, w['ln2_g'], w['ln2_b'])
    return x, c_kv, k_pe, s_new


def setup_inputs(seed: int = 0) -> dict:
    key = jax.random.key(seed)
    ks = jax.random.split(key, 32)
    n_pages = PAST_LEN // PAGE_SIZE
    n_phys = (DEC_BATCH * n_pages * 5) // 4

    def nrm(k, shape, s):
        return jax.random.normal(k, shape, jnp.float32) * s

    def gain(k, n):
        return 1.0 + nrm(k, (DEPTH, n), 0.01)

    in_scales = (1.0, 1.0, 1.0, 1.0, 1.0, BETA, 1.0, 1.0, 1.0)
    seg_keys = jax.random.split(ks[8], len(IN_SIZES))
    w_in = jnp.concatenate([nrm(k, (DEPTH, D_MODEL, n), s * D_MODEL ** -0.5)
                            for k, n, s in zip(seg_keys, IN_SIZES, in_scales)], axis=-1)
    page_table = jax.random.permutation(ks[7], n_phys)[: DEC_BATCH * n_pages]
    page_table = page_table.reshape(DEC_BATCH, n_pages).astype(jnp.int32)
    return {
        'x_prompt': nrm(ks[0], (BATCH, SEQ, D_MODEL), 1.0),
        'x_sample': nrm(ks[1], (DEC_BATCH, DEC_SEQ, D_MODEL), 1.0),
        'c_prompt': nrm(ks[2], (BATCH, D_MODEL), 1.0),
        'c_sample': nrm(ks[3], (DEC_BATCH, D_MODEL), 1.0),
        'cache_ckv': nrm(ks[4], (DEPTH, n_phys, PAGE_SIZE, KV_LORA), 1.0),
        'cache_kpe': nrm(ks[5], (DEPTH, n_phys, PAGE_SIZE, QK_ROPE), 1.0),
        'state_ret': nrm(ks[6], (DEPTH, DEC_BATCH, RET_HEADS, RET_DK, RET_DV), 0.3),
        'page_table': page_table,
        'w_ada': nrm(ks[9], (DEPTH, D_MODEL, 6 * D_MODEL), 0.2 * D_MODEL ** -0.5),
        'b_ada': nrm(ks[10], (DEPTH, 6 * D_MODEL), 0.01),
        'w_in': w_in,
        'g_qnorm': gain(ks[11], Q_LORA),
        'g_kvnorm': gain(ks[12], KV_LORA),
        'w_uq': nrm(ks[13], (DEPTH, Q_LORA, MLA_HEADS, QK_NOPE + QK_ROPE), Q_LORA ** -0.5),
        'w_uk': nrm(ks[14], (DEPTH, KV_LORA, MLA_HEADS, QK_NOPE), KV_LORA ** -0.5),
        'w_uv': nrm(ks[15], (DEPTH, KV_LORA, MLA_HEADS, V_HEAD), BETA * KV_LORA ** -0.5),
        'w_o': nrm(ks[16], (DEPTH, D_MODEL, D_MODEL), BETA * D_MODEL ** -0.5),
        'ln1_g': gain(ks[17], D_MODEL),
        'ln1_b': nrm(ks[18], (DEPTH, D_MODEL), 0.01),
        'w_up': nrm(ks[19], (DEPTH, D_MODEL, D_FF), D_MODEL ** -0.5),
        'w_down': nrm(ks[20], (DEPTH, D_FF, D_MODEL), BETA * D_FF ** -0.5),
        'ln2_g': gain(ks[21], D_MODEL),
        'ln2_b': nrm(ks[22], (DEPTH, D_MODEL), 0.01),
    }


def reference(x_prompt, x_sample, c_prompt, c_sample, cache_ckv, cache_kpe, state_ret, page_table,
              w_ada, b_ada, w_in, g_qnorm, g_kvnorm, w_uq, w_uk, w_uv, w_o,
              ln1_g, ln1_b, w_up, w_down, ln2_g, ln2_b):
    B, S, _ = x_prompt.shape
    DB, T, _ = x_sample.shape
    pos_prompt = jnp.arange(S)
    ret_zero = jnp.zeros((B, RET_HEADS, RET_DK, RET_DV), x_prompt.dtype)
    xp, xs = x_prompt, x_sample
    ckv_p, kpe_p, ret_p, ckv_s, kpe_s, ret_s = [], [], [], [], [], []
    for l in range(DEPTH):
        w = {'w_ada': w_ada[l], 'b_ada': b_ada[l], 'w_in': w_in[l], 'g_qnorm': g_qnorm[l],
             'g_kvnorm': g_kvnorm[l], 'w_uq': w_uq[l], 'w_uk': w_uk[l], 'w_uv': w_uv[l], 'w_o': w_o[l],
             'ln1_g': ln1_g[l], 'ln1_b': ln1_b[l], 'w_up': w_up[l], 'w_down': w_down[l],
             'ln2_g': ln2_g[l], 'ln2_b': ln2_b[l]}
        xp, ck, kp, sr = decoder_layer(xp, c_prompt, pos_prompt, attend_prompt, ret_zero,
                                       min(RET_CHUNK, S), w)
        ckv_p.append(ck); kpe_p.append(kp); ret_p.append(sr)
        attend_s, past_len = make_attend_sample(cache_ckv[l], cache_kpe[l], page_table)
        pos_sample = past_len + jnp.arange(T)
        xs, ck, kp, sr = decoder_layer(xs, c_sample, pos_sample, attend_s, state_ret[l], T, w)
        ckv_s.append(ck); kpe_s.append(kp); ret_s.append(sr)
    return (xp, xs, jnp.stack(ckv_p), jnp.stack(kpe_p), jnp.stack(ret_p),
            jnp.stack(ckv_s), jnp.stack(kpe_s), jnp.stack(ret_s))
```

```python
import functools

import jax
import jax.numpy as jnp
from jax import lax
from jax.experimental import pallas as pl
from jax.experimental.pallas import tpu as pltpu

D_MODEL = 1024
DEPTH = 1
PAGE_SIZE = 128
MLA_HEADS = 8
Q_LORA = 384
KV_LORA = 256
QK_NOPE = 128
QK_ROPE = 64
V_HEAD = D_MODEL // MLA_HEADS
MLA_SCALE = (QK_NOPE + QK_ROPE) ** -0.5
RET_HEADS = 8
RET_DK = 128
RET_DV = D_MODEL // RET_HEADS
RET_CHUNK = 128
D_FF = 4 * D_MODEL
ROPE_BASE = 10000.0
LN_EPS = 1e-5
RMS_EPS = 1e-6
NEG = -1e30
ALPHA = (2.0 * DEPTH) ** 0.25

LANES = 128
V7X_VMEM_BYTES = 64 * 1024 * 1024
VMEM_LIMIT = V7X_VMEM_BYTES - 8 * 1024 * 1024

F32 = jnp.float32
BF16 = jnp.bfloat16


def _params(sem):
    return pltpu.CompilerParams(dimension_semantics=sem, vmem_limit_bytes=VMEM_LIMIT)


def _resident(shape):
    nd = len(shape)
    return pl.BlockSpec(shape, lambda *_: (0,) * nd, pipeline_mode=pl.Buffered(1))


def _dot(a, b):
    return jnp.dot(a, b, preferred_element_type=F32)


def _dot_nt(a, b):
    return lax.dot_general(a, b, (((1,), (1,)), ((), ())), preferred_element_type=F32)


def _rope(x, cos, sin_signed, half):
    lanes = x.shape[-1]
    if 2 * half == lanes:
        partner = pltpu.roll(x, half, axis=1)
    else:
        lane = lax.broadcasted_iota(jnp.int32, x.shape, 1)
        first = (lane & (2 * half - 1)) < half
        partner = jnp.where(first, pltpu.roll(x, lanes - half, axis=1), pltpu.roll(x, half, axis=1))
    return x * cos + partner * sin_signed


def _layer_norm(v, g, b):
    mu = jnp.mean(v, axis=-1, keepdims=True)
    vc = v - mu
    var = jnp.mean(vc * vc, axis=-1, keepdims=True)
    return vc * lax.rsqrt(var + LN_EPS) * g + b


def _ada_kernel(c_ref, w_ref, b_ref, o_ref):
    c = c_ref[...]
    s = (c * jax.nn.sigmoid(c)).astype(BF16)
    o_ref[...] = _dot(s, w_ref[...].astype(BF16)) + b_ref[...]


def _ada(c_all, w_ada, b_ada):
    rows = c_all.shape[0]
    tn = D_MODEL
    return pl.pallas_call(
        _ada_kernel,
        grid=(6 * D_MODEL // tn,),
        in_specs=[pl.BlockSpec((rows, D_MODEL), lambda j: (0, 0)),
                  pl.BlockSpec((D_MODEL, tn), lambda j: (0, j)),
                  pl.BlockSpec((1, tn), lambda j: (0, j))],
        out_specs=pl.BlockSpec((rows, tn), lambda j: (0, j)),
        out_shape=jax.ShapeDtypeStruct((rows, 6 * D_MODEL), F32),
        compiler_params=_params(("parallel",)),
        name="ada",
    )(c_all, w_ada, b_ada)


def _inproj_kernel(x_ref, mod_ref, cosa_ref, sina_ref, cosb_ref, sinb_ref,
                   wh_ref, wr_ref, gq_ref, gkv_ref, wqn_ref, wqp_ref, wuk_ref,
                   ckv_ref, kpe_ref, kc_ref, kp_ref, qlat_ref, qpe_ref,
                   rq_ref, rk_ref, rv_ref, ga_ref, gr_ref):
    D = D_MODEL
    x = x_ref[0]
    sh1 = mod_ref[0, :, 0:D]
    sc1 = mod_ref[0, :, D:2 * D]
    h = (x * (1.0 + sc1) + sh1).astype(BF16)

    yh = _dot(h, wh_ref[...])
    pq = yh[:, 0:Q_LORA]
    pkv = yh[:, Q_LORA:Q_LORA + KV_LORA]
    pkr = yh[:, Q_LORA + KV_LORA:Q_LORA + KV_LORA + LANES]
    qn = (pq * lax.rsqrt(jnp.mean(pq * pq, axis=-1, keepdims=True) + RMS_EPS) * gq_ref[...]).astype(BF16)
    ckv = pkv * lax.rsqrt(jnp.mean(pkv * pkv, axis=-1, keepdims=True) + RMS_EPS) * gkv_ref[...]
    ckv_ref[0] = ckv
    kc_ref[0] = ckv.astype(BF16)
    cosa = cosa_ref[...]
    sina = sina_ref[...]
    kpe = _rope(pkr, cosa, sina, QK_ROPE // 2)[:, 0:QK_ROPE]
    kpe_ref[0] = kpe
    kp_ref[0] = kpe.astype(BF16)

    qpe = _dot(qn, wqp_ref[...])
    for j in range(MLA_HEADS * QK_ROPE // LANES):
        blk = _rope(qpe[:, j * LANES:(j + 1) * LANES], cosa, sina, QK_ROPE // 2) * MLA_SCALE
        for hh in range(LANES // QK_ROPE):
            head = j * (LANES // QK_ROPE) + hh
            qpe_ref[0, head] = blk[:, hh * QK_ROPE:(hh + 1) * QK_ROPE].astype(BF16)
    qnope = _dot(qn, wqn_ref[...])
    for head in range(MLA_HEADS):
        qh = qnope[:, head * QK_NOPE:(head + 1) * QK_NOPE].astype(BF16)
        qlat_ref[0, head] = (_dot(qh, wuk_ref[head]) * MLA_SCALE).astype(BF16)

    cosb = cosb_ref[...]
    sinb = sinb_ref[...]
    rq = _dot(h, wr_ref[:, 0:D])
    rk = _dot(h, wr_ref[:, D:2 * D])
    for head in range(RET_HEADS):
        sl = slice(head * RET_DK, (head + 1) * RET_DK)
        rq_ref[0, :, sl] = _rope(rq[:, sl], cosb, sinb, RET_DK // 2).astype(rq_ref.dtype)
        rk_ref[0, :, sl] = (_rope(rk[:, sl], cosb, sinb, RET_DK // 2) * (RET_DK ** -0.5)).astype(rk_ref.dtype)
    rv_ref[0] = _dot(h, wr_ref[:, 2 * D:3 * D]).astype(rv_ref.dtype)
    rg = _dot(h, wr_ref[:, 3 * D:4 * D])
    ga_ref[0] = jax.nn.sigmoid(_dot(h, wr_ref[:, 4 * D:5 * D])).astype(BF16)
    gr = jax.nn.sigmoid(_dot(h, wr_ref[:, 5 * D:6 * D]))
    gr_ref[0] = (gr * (rg * jax.nn.sigmoid(rg))).astype(BF16)


def _inproj(x3, mod3, tabs, w, *, tm, act_dtype):
    G, Sg, D = x3.shape
    R = mod3.shape[1]
    assert Sg % tm == 0 and R in (1, tm) and (R == 1 or Sg == tm)
    row = lambda g, i: (g, i, 0)
    head_row = lambda g, i: (g, 0, i, 0)
    tab = pl.BlockSpec((tm, LANES), lambda g, i: (i, 0))
    in_specs = [
        pl.BlockSpec((1, tm, D), row),
        pl.BlockSpec((1, R, 2 * D), lambda g, i: (g, 0, 0)),
        tab, tab, tab, tab,
        _resident(w["w_head"].shape), _resident(w["w_r"].shape),
        _resident(w["g_q"].shape), _resident(w["g_kv"].shape),
        _resident(w["w_uq_nope"].shape), _resident(w["w_uq_pe"].shape), _resident(w["w_ukT"].shape),
    ]
    out_shape = [
        jax.ShapeDtypeStruct((G, Sg, KV_LORA), F32), jax.ShapeDtypeStruct((G, Sg, QK_ROPE), F32),
        jax.ShapeDtypeStruct((G, Sg, KV_LORA), BF16), jax.ShapeDtypeStruct((G, Sg, QK_ROPE), BF16),
        jax.ShapeDtypeStruct((G, MLA_HEADS, Sg, KV_LORA), BF16),
        jax.ShapeDtypeStruct((G, MLA_HEADS, Sg, QK_ROPE), BF16),
        jax.ShapeDtypeStruct((G, Sg, D), act_dtype), jax.ShapeDtypeStruct((G, Sg, D), act_dtype),
        jax.ShapeDtypeStruct((G, Sg, D), act_dtype),
        jax.ShapeDtypeStruct((G, Sg, D), BF16), jax.ShapeDtypeStruct((G, Sg, D), BF16),
    ]
    out_specs = [
        pl.BlockSpec((1, tm, KV_LORA), row), pl.BlockSpec((1, tm, QK_ROPE), row),
        pl.BlockSpec((1, tm, KV_LORA), row), pl.BlockSpec((1, tm, QK_ROPE), row),
        pl.BlockSpec((1, MLA_HEADS, tm, KV_LORA), head_row),
        pl.BlockSpec((1, MLA_HEADS, tm, QK_ROPE), head_row),
        pl.BlockSpec((1, tm, D), row), pl.BlockSpec((1, tm, D), row), pl.BlockSpec((1, tm, D), row),
        pl.BlockSpec((1, tm, D), row), pl.BlockSpec((1, tm, D), row),
    ]
    return pl.pallas_call(
        _inproj_kernel,
        grid=(G, Sg // tm),
        in_specs=in_specs, out_specs=out_specs, out_shape=out_shape,
        compiler_params=_params(("parallel", "parallel")),
        name="inproj",
    )(x3, mod3, *tabs, w["w_head"], w["w_r"], w["g_q"], w["g_kv"],
      w["w_uq_nope"], w["w_uq_pe"], w["w_ukT"])


def _mla_prompt_kernel(qlat_ref, qpe_ref, kc_ref, kp_ref, wuv_ref, o_ref, m_ref, l_ref, acc_ref, *, tq):
    qi = pl.program_id(1)
    rows = MLA_HEADS * tq
    q1 = qlat_ref[0].reshape(rows, KV_LORA)
    q2 = qpe_ref[0].reshape(rows, QK_ROPE)
    m_ref[...] = jnp.full(m_ref.shape, -jnp.inf, F32)
    l_ref[...] = jnp.zeros(l_ref.shape, F32)
    acc_ref[...] = jnp.zeros(acc_ref.shape, F32)

    def tile(j, masked):
        start = pl.multiple_of(j * tq, tq)
        kc = kc_ref[0, pl.ds(start, tq), :]
        kp = kp_ref[0, pl.ds(start, tq), :]
        s = _dot_nt(q1, kc) + _dot_nt(q2, kp)
        if masked:
            tok = lax.broadcasted_iota(jnp.int32, s.shape, 0) & (tq - 1)
            col = lax.broadcasted_iota(jnp.int32, s.shape, 1)
            s = jnp.where(col <= tok, s, NEG)
        m_prev = m_ref[...]
        m_new = jnp.maximum(m_prev, jnp.max(s, axis=-1, keepdims=True))
        a = jnp.exp(m_prev - m_new)
        p = jnp.exp(s - m_new)
        l_ref[...] = a * l_ref[...] + jnp.sum(p, axis=-1, keepdims=True)
        acc_ref[...] = a * acc_ref[...] + _dot(p.astype(BF16), kc)
        m_ref[...] = m_new

    def body(j, carry):
        tile(j, False)
        return carry

    lax.fori_loop(0, qi, body, 0)
    tile(qi, True)
    o = acc_ref[...] * (1.0 / l_ref[...])
    for head in range(MLA_HEADS):
        oh = o[head * tq:(head + 1) * tq].astype(BF16)
        o_ref[0, :, head * V_HEAD:(head + 1) * V_HEAD] = _dot(oh, wuv_ref[head]).astype(BF16)


def _mla_prompt(qlat, qpe, kc, kp, w_uv, *, tq):
    B, Hh, S, _ = qlat.shape
    rows = Hh * tq
    return pl.pallas_call(
        functools.partial(_mla_prompt_kernel, tq=tq),
        grid=(B, S // tq),
        in_specs=[pl.BlockSpec((1, Hh, tq, KV_LORA), lambda b, i: (b, 0, i, 0)),
                  pl.BlockSpec((1, Hh, tq, QK_ROPE), lambda b, i: (b, 0, i, 0)),
                  pl.BlockSpec((1, S, KV_LORA), lambda b, i: (b, 0, 0)),
                  pl.BlockSpec((1, S, QK_ROPE), lambda b, i: (b, 0, 0)),
                  _resident(w_uv.shape)],
        out_specs=pl.BlockSpec((1, tq, D_MODEL), lambda b, i: (b, i, 0)),
        out_shape=jax.ShapeDtypeStruct((B, S, D_MODEL), BF16),
        scratch_shapes=[pltpu.VMEM((rows, 1), F32), pltpu.VMEM((rows, 1), F32),
                        pltpu.VMEM((rows, KV_LORA), F32)],
        compiler_params=_params(("parallel", "parallel")),
        name="mla_prompt",
    )(qlat, qpe, kc, kp, w_uv)


def _mla_sample_kernel(pt_ref, q1_ref, q2_ref, cn_ref, kn_ref, *refs, pages, tokens):
    del pt_ref
    ck_refs = refs[:pages]
    kp_refs = refs[pages:2 * pages]
    o_ref, m_ref, l_ref, acc_ref = refs[2 * pages:]
    g = pl.program_id(1)

    @pl.when(g == 0)
    def _():
        m_ref[...] = jnp.full(m_ref.shape, -jnp.inf, F32)
        l_ref[...] = jnp.zeros(l_ref.shape, F32)
        acc_ref[...] = jnp.zeros(acc_ref.shape, F32)

    q1 = q1_ref[0]
    q2 = q2_ref[0]
    kbs = [ck_refs[j][...].astype(BF16) for j in range(pages)]
    s = jnp.concatenate(
        [_dot_nt(q1, kbs[j]) + _dot_nt(q2, kp_refs[j][...].astype(BF16)) for j in range(pages)], axis=1)
    m_prev = m_ref[...]
    m_new = jnp.maximum(m_prev, jnp.max(s, axis=-1, keepdims=True))
    a = jnp.exp(m_prev - m_new)
    p = jnp.exp(s - m_new)
    l_ref[...] = a * l_ref[...] + jnp.sum(p, axis=-1, keepdims=True)
    pv = _dot(p[:, 0:PAGE_SIZE].astype(BF16), kbs[0])
    for j in range(1, pages):
        pv = pv + _dot(p[:, j * PAGE_SIZE:(j + 1) * PAGE_SIZE].astype(BF16), kbs[j])
    acc_ref[...] = a * acc_ref[...] + pv
    m_ref[...] = m_new

    @pl.when(g == pl.num_programs(1) - 1)
    def _():
        q1f = q1.astype(F32)
        q2f = q2.astype(F32)
        cn = cn_ref[0]
        kn = kn_ref[0]
        tok = lax.broadcasted_iota(jnp.int32, m_ref.shape, 0) & (tokens - 1)
        cols = []
        for t in range(tokens):
            sc = (jnp.sum(q1f * cn[t:t + 1, :], axis=-1, keepdims=True)
                  + jnp.sum(q2f * kn[t:t + 1, :], axis=-1, keepdims=True))
            cols.append(jnp.where(tok >= t, sc, NEG))
        m_old = m_ref[...]
        m_fin = m_old
        for sc in cols:
            m_fin = jnp.maximum(m_fin, sc)
        a2 = jnp.exp(m_old - m_fin)
        l_fin = a2 * l_ref[...]
        acc = a2 * acc_ref[...]
        for t, sc in enumerate(cols):
            pt = jnp.exp(sc - m_fin)
            l_fin = l_fin + pt
            acc = acc + pt * cn[t:t + 1, :]
        o_ref[0] = acc * (1.0 / l_fin)


def _mla_sample(page_table, q1, q2, cnew, knew, cache_ckv, cache_kpe, *, pages):
    DB, n_pages = page_table.shape
    rows = q1.shape[1]
    tokens = cnew.shape[1]
    assert n_pages % pages == 0 and tokens & (tokens - 1) == 0

    def page_spec(width, j):
        return pl.BlockSpec((None, None, PAGE_SIZE, width),
                            lambda b, g, pt: (0, pt[b, g * pages + j], 0, 0))

    per_batch = lambda b, g, pt: (b, 0, 0)
    grid_spec = pltpu.PrefetchScalarGridSpec(
        num_scalar_prefetch=1,
        grid=(DB, n_pages // pages),
        in_specs=[pl.BlockSpec((1, rows, KV_LORA), per_batch),
                  pl.BlockSpec((1, rows, QK_ROPE), per_batch),
                  pl.BlockSpec((1, tokens, KV_LORA), per_batch),
                  pl.BlockSpec((1, tokens, QK_ROPE), per_batch)]
                 + [page_spec(KV_LORA, j) for j in range(pages)]
                 + [page_spec(QK_ROPE, j) for j in range(pages)],
        out_specs=pl.BlockSpec((1, rows, KV_LORA), per_batch),
        scratch_shapes=[pltpu.VMEM((rows, 1), F32), pltpu.VMEM((rows, 1), F32),
                        pltpu.VMEM((rows, KV_LORA), F32)],
    )
    return pl.pallas_call(
        functools.partial(_mla_sample_kernel, pages=pages, tokens=tokens),
        grid_spec=grid_spec,
        out_shape=jax.ShapeDtypeStruct((DB, rows, KV_LORA), F32),
        compiler_params=_params(("parallel", "arbitrary")),
        name="mla_sample",
    )(page_table, q1, q2, cnew, knew, *([cache_ckv] * pages), *([cache_kpe] * pages))


def _retention_kernel(*refs, T, L, has_s0):
    C = RET_CHUNK
    padded = T < C
    if has_s0:
        q_ref, k_ref, v_ref, s0_ref, y_ref, s_ref = refs[:6]
        rest = refs[6:]
    else:
        q_ref, k_ref, v_ref, y_ref, s_ref = refs[:5]
        rest = refs[5:]
    head = (jnp.zeros((1, LANES), jnp.int32) + pl.program_id(1)).astype(F32)
    lg = jnp.log(1.0 - jnp.exp2(-5.0 - head))
    ri = lax.broadcasted_iota(jnp.int32, (C, C), 0).astype(F32)
    ci = lax.broadcasted_iota(jnp.int32, (C, C), 1).astype(F32)
    diff = ri - ci
    causal = diff >= 0.0
    dmask = jnp.where(causal, jnp.exp(jnp.where(causal, diff, 0.0) * lg), 0.0)
    q_decay = jnp.exp((ri + 1.0) * lg)
    k_decay = jnp.exp((L - 1.0 - ri) * lg)
    chunk_decay = jnp.exp(float(L) * lg)

    def chunk(qc, kc, vc, s):
        scores = _dot_nt(qc, kc) * dmask
        inner = _dot(scores.astype(BF16), vc)
        cross = _dot(qc, s.astype(BF16)) * q_decay
        kd_t = (kc.astype(F32) * k_decay).T.astype(BF16)
        s_new = chunk_decay * s + _dot(kd_t, vc)
        y = inner + cross
        mu = jnp.mean(y, axis=-1, keepdims=True)
        yc = y - mu
        var = jnp.mean(yc * yc, axis=-1, keepdims=True)
        return yc * lax.rsqrt(var + LN_EPS), s_new

    s0 = s0_ref[0, 0] if has_s0 else jnp.zeros((RET_DK, RET_DV), F32)
    if padded:
        qp_ref, kp_ref, vp_ref = rest
        for src, dst in ((q_ref, qp_ref), (k_ref, kp_ref), (v_ref, vp_ref)):
            dst[...] = jnp.zeros(dst.shape, F32)
            dst[0:T, :] = src[0].astype(F32)
        yn, s_fin = chunk(qp_ref[...].astype(BF16), kp_ref[...].astype(BF16), vp_ref[...].astype(BF16), s0)
        y_ref[0] = yn[0:T, :].astype(y_ref.dtype)
    else:
        def body(c, s):
            rows = pl.ds(pl.multiple_of(c * C, C), C)
            yn, s_new = chunk(q_ref[0, rows, :].astype(BF16), k_ref[0, rows, :].astype(BF16),
                              v_ref[0, rows, :].astype(BF16), s)
            y_ref[0, rows, :] = yn.astype(y_ref.dtype)
            return s_new
        s_fin = lax.fori_loop(0, T // C, body, s0)
    s_ref[0, 0] = s_fin


def _retention(rq, rk, rv, s0, *, L, out_dtype):
    B, T, _ = rq.shape
    has_s0 = s0 is not None
    padded = T < RET_CHUNK
    assert padded or T % RET_CHUNK == 0
    tok = pl.BlockSpec((1, T, RET_DK), lambda b, h: (b, 0, h))
    state = pl.BlockSpec((1, 1, RET_DK, RET_DV), lambda b, h: (b, h, 0, 0))
    in_specs = [tok, tok, tok] + ([state] if has_s0 else [])
    args = (rq, rk, rv) + ((s0,) if has_s0 else ())
    return pl.pallas_call(
        functools.partial(_retention_kernel, T=T, L=L, has_s0=has_s0),
        grid=(B, RET_HEADS),
        in_specs=in_specs,
        out_specs=[tok, state],
        out_shape=[jax.ShapeDtypeStruct((B, T, D_MODEL), out_dtype),
                   jax.ShapeDtypeStruct((B, RET_HEADS, RET_DK, RET_DV), F32)],
        scratch_shapes=[pltpu.VMEM((RET_CHUNK, RET_DK), F32)] * 3 if padded else [],
        compiler_params=_params(("parallel", "parallel")),
        name="retention",
    )(*args)


def _post_kernel(*refs, apply_uv):
    if apply_uv:
        (x_ref, att_ref, y_ref, ga_ref, gr_ref, mod_ref, wuv_ref, wo_ref, wup_ref, wdn_ref,
         g1_ref, b1_ref, g2_ref, b2_ref, o_ref) = refs
    else:
        (x_ref, att_ref, y_ref, ga_ref, gr_ref, mod_ref, wo_ref, wup_ref, wdn_ref,
         g1_ref, b1_ref, g2_ref, b2_ref, o_ref) = refs
    D = D_MODEL
    x = x_ref[0]
    if apply_uv:
        o_a = jnp.concatenate([_dot(att_ref[0, head].astype(BF16), wuv_ref[head])
                               for head in range(MLA_HEADS)], axis=1)
    else:
        o_a = att_ref[0].astype(F32)
    mixv = ga_ref[0].astype(F32) * o_a + gr_ref[0].astype(F32) * y_ref[0].astype(F32)
    mix = _dot(mixv.astype(BF16), wo_ref[...])
    g1 = mod_ref[0, :, 2 * D:3 * D]
    sh2 = mod_ref[0, :, 3 * D:4 * D]
    sc2 = mod_ref[0, :, 4 * D:5 * D]
    g2 = mod_ref[0, :, 5 * D:6 * D]
    x1 = _layer_norm(ALPHA * x + (1.0 + g1) * mix, g1_ref[...], b1_ref[...])
    h2 = (x1 * (1.0 + sc2) + sh2).astype(BF16)
    m = None
    for c in range(D_FF // D):
        u = jnp.maximum(_dot(h2, wup_ref[:, c * D:(c + 1) * D]), 0.0)
        part = _dot((u * u).astype(BF16), wdn_ref[c * D:(c + 1) * D, :])
        m = part if m is None else m + part
    o_ref[0] = _layer_norm(ALPHA * x1 + (1.0 + g2) * m, g2_ref[...], b2_ref[...])


def _post(x3, att, y3, ga, gr, mod3, w, *, tm, apply_uv):
    G, Sg, D = x3.shape
    R = mod3.shape[1]
    row = lambda g, i: (g, i, 0)
    act = pl.BlockSpec((1, tm, D), row)
    if apply_uv:
        att_spec = pl.BlockSpec((1, MLA_HEADS, tm, KV_LORA), lambda g, i: (g, 0, i, 0))
    else:
        att_spec = act
    in_specs = [act, att_spec, act, act, act, pl.BlockSpec((1, R, 6 * D), lambda g, i: (g, 0, 0))]
    args = [x3, att, y3, ga, gr, mod3]
    names = (["w_uv"] if apply_uv else []) + ["w_o", "w_up", "w_down", "ln1_g", "ln1_b", "ln2_g", "ln2_b"]
    for n in names:
        in_specs.append(_resident(w[n].shape))
        args.append(w[n])
    return pl.pallas_call(
        functools.partial(_post_kernel, apply_uv=apply_uv),
        grid=(G, Sg // tm),
        in_specs=in_specs,
        out_specs=act,
        out_shape=jax.ShapeDtypeStruct((G, Sg, D), F32),
        compiler_params=_params(("parallel", "parallel")),
        name="post",
    )(*args)


def _rope_tables(pos, dim):
    half = dim // 2
    inv = ROPE_BASE ** (-jnp.arange(half, dtype=F32) / half)
    ang = pos.astype(F32)[:, None] * inv[None, :]
    cos, sin = jnp.cos(ang), jnp.sin(ang)
    reps = LANES // dim
    return (jnp.tile(jnp.concatenate([cos, cos], -1), (1, reps)),
            jnp.tile(jnp.concatenate([-sin, sin], -1), (1, reps)))


def kernel(x_prompt, x_sample, c_prompt, c_sample, cache_ckv, cache_kpe, state_ret, page_table,
           w_ada, b_ada, w_in, g_qnorm, g_kvnorm, w_uq, w_uk, w_uv, w_o,
           ln1_g, ln1_b, w_up, w_down, ln2_g, ln2_b):
    B, S, D = x_prompt.shape
    DB, T, _ = x_sample.shape
    n_pages = page_table.shape[1]
    past_len = n_pages * PAGE_SIZE
    assert DEPTH == 1 and w_in.shape[0] == 1
    l = 0
    n_head = Q_LORA + KV_LORA + QK_ROPE

    w_kr = w_in[l][:, Q_LORA + KV_LORA:n_head]
    w = {
        "w_head": jnp.concatenate([w_in[l][:, :n_head], w_kr], axis=1).astype(BF16),
        "w_r": w_in[l][:, n_head:].astype(BF16),
        "g_q": g_qnorm[l][None, :], "g_kv": g_kvnorm[l][None, :],
        "w_uq_nope": w_uq[l][:, :, :QK_NOPE].reshape(Q_LORA, MLA_HEADS * QK_NOPE).astype(BF16),
        "w_uq_pe": w_uq[l][:, :, QK_NOPE:].reshape(Q_LORA, MLA_HEADS * QK_ROPE).astype(BF16),
        "w_ukT": jnp.transpose(w_uk[l], (1, 2, 0)).astype(BF16),
        "w_uv": jnp.transpose(w_uv[l], (1, 0, 2)).astype(BF16),
        "w_o": w_o[l].astype(BF16), "w_up": w_up[l].astype(BF16), "w_down": w_down[l].astype(BF16),
        "ln1_g": ln1_g[l][None, :], "ln1_b": ln1_b[l][None, :],
        "ln2_g": ln2_g[l][None, :], "ln2_b": ln2_b[l][None, :],
    }

    n_c = B + DB
    pad = (-n_c) % 16
    c_all = jnp.concatenate([c_prompt, c_sample, jnp.zeros((pad, D), F32)], axis=0)
    ada = _ada(c_all, w_ada[l], b_ada[l][None, :])
    mod_p = ada[:B].reshape(B, 1, 6 * D)
    mod_s = jnp.repeat(ada[B:n_c], T, axis=0).reshape(1, DB * T, 6 * D)

    pos_p = jnp.arange(S)
    tabs_p = _rope_tables(pos_p, QK_ROPE) + _rope_tables(pos_p, RET_DK)
    (ckv_p, kpe_p, kc_p, kp_p, qlat_p, qpe_p, rq_p, rk_p, rv_p, ga_p, gr_p) = _inproj(
        x_prompt, mod_p, tabs_p, w, tm=256, act_dtype=BF16)
    oa_p = _mla_prompt(qlat_p, qpe_p, kc_p, kp_p, w["w_uv"], tq=256)
    y_p, ret_p = _retention(rq_p, rk_p, rv_p, None, L=min(RET_CHUNK, S), out_dtype=BF16)
    out_p = _post(x_prompt, oa_p, y_p, ga_p, gr_p, mod_p, w, tm=256, apply_uv=False)

    n_s = DB * T
    pos_s = jnp.tile(past_len + jnp.arange(T), DB)
    tabs_s = _rope_tables(pos_s, QK_ROPE) + _rope_tables(pos_s, RET_DK)
    xs = x_sample.reshape(1, n_s, D)
    (ckv_s, kpe_s, _, _, qlat_s, qpe_s, rq_s, rk_s, rv_s, ga_s, gr_s) = _inproj(
        xs, mod_s, tabs_s, w, tm=n_s, act_dtype=F32)

    def to_batch_rows(q):
        d = q.shape[-1]
        return q[0].reshape(MLA_HEADS, DB, T, d).transpose(1, 0, 2, 3).reshape(DB, MLA_HEADS * T, d)

    olat_s = _mla_sample(page_table, to_batch_rows(qlat_s), to_batch_rows(qpe_s),
                         ckv_s.reshape(DB, T, KV_LORA), kpe_s.reshape(DB, T, QK_ROPE),
                         cache_ckv, cache_kpe, pages=16)
    olat_s = olat_s.reshape(DB, MLA_HEADS, T, KV_LORA).transpose(1, 0, 2, 3).reshape(1, MLA_HEADS, n_s, KV_LORA)
    y_s, ret_s = _retention(rq_s.reshape(DB, T, D), rk_s.reshape(DB, T, D), rv_s.reshape(DB, T, D),
                            state_ret[l], L=T, out_dtype=F32)
    out_s = _post(xs, olat_s, y_s.reshape(1, n_s, D), ga_s, gr_s, mod_s, w, tm=n_s, apply_uv=True)

    return (out_p, out_s.reshape(DB, T, D),
            ckv_p[None], kpe_p[None], ret_p[None],
            ckv_s.reshape(1, DB, T, KV_LORA), kpe_s.reshape(1, DB, T, QK_ROPE), ret_s[None])
```

```python
import functools
import math

import jax
import jax.numpy as jnp
from jax import lax
from jax.experimental import pallas as pl
from jax.experimental.pallas import tpu as pltpu

D_MODEL = 1024
DEPTH = 1
PAGE_SIZE = 128
MLA_HEADS = 8
Q_LORA = 384
KV_LORA = 256
QK_NOPE = 128
QK_ROPE = 64
V_HEAD = D_MODEL // MLA_HEADS
MLA_SCALE = (QK_NOPE + QK_ROPE) ** -0.5
LOG2E = 1.4426950408889634
Q_SCALE = MLA_SCALE * LOG2E
RET_HEADS = 8
RET_DK = 128
RET_DV = D_MODEL // RET_HEADS
RET_CHUNK = 128
D_FF = 4 * D_MODEL
ROPE_BASE = 10000.0
LN_EPS = 1e-5
RMS_EPS = 1e-6
NEG = -1e30
ALPHA = (2.0 * DEPTH) ** 0.25

SCORE_LOOKAHEAD = 3
LANES = 128
V7X_VMEM_BYTES = 64 * 1024 * 1024
VMEM_LIMIT = V7X_VMEM_BYTES - 8 * 1024 * 1024

F32 = jnp.float32
BF16 = jnp.bfloat16


def _params(sem):
    return pltpu.CompilerParams(dimension_semantics=sem, vmem_limit_bytes=VMEM_LIMIT)


def _resident(shape):
    nd = len(shape)
    return pl.BlockSpec(shape, lambda *_: (0,) * nd, pipeline_mode=pl.Buffered(1))


def _dot(a, b):
    return jnp.dot(a, b, preferred_element_type=F32)


def _dot_nt(a, b):
    return lax.dot_general(a, b, (((1,), (1,)), ((), ())), preferred_element_type=F32)


def _rope(x, cos, sin_signed, half):
    lanes = x.shape[-1]
    if 2 * half == lanes:
        partner = pltpu.roll(x, half, axis=1)
    else:
        lane = lax.broadcasted_iota(jnp.int32, x.shape, 1)
        first = (lane & (2 * half - 1)) < half
        partner = jnp.where(first, pltpu.roll(x, lanes - half, axis=1), pltpu.roll(x, half, axis=1))
    return x * cos + partner * sin_signed


def _layer_norm(v, g, b):
    mu = jnp.mean(v, axis=-1, keepdims=True)
    vc = v - mu
    var = jnp.mean(vc * vc, axis=-1, keepdims=True)
    return vc * lax.rsqrt(var + LN_EPS) * g + b


def _ada_kernel(c_ref, w_ref, b_ref, o_ref):
    c = c_ref[...]
    s = (c * jax.nn.sigmoid(c)).astype(BF16)
    o_ref[...] = _dot(s, w_ref[...].astype(BF16)) + b_ref[...]


def _ada(c_all, w_ada, b_ada):
    rows = c_all.shape[0]
    tn = D_MODEL
    return pl.pallas_call(
        _ada_kernel,
        grid=(6 * D_MODEL // tn,),
        in_specs=[pl.BlockSpec((rows, D_MODEL), lambda j: (0, 0)),
                  pl.BlockSpec((D_MODEL, tn), lambda j: (0, j)),
                  pl.BlockSpec((1, tn), lambda j: (0, j))],
        out_specs=pl.BlockSpec((rows, tn), lambda j: (0, j)),
        out_shape=jax.ShapeDtypeStruct((rows, 6 * D_MODEL), F32),
        compiler_params=_params(("parallel",)),
        name="ada",
    )(c_all, w_ada, b_ada)


def _inproj_kernel(x_ref, mod_ref, cosa_ref, sina_ref, cosb_ref, sinb_ref,
                   wh_ref, wr_ref, gq_ref, gkv_ref, wqn_ref, wqp_ref, wuk_ref,
                   ckv_ref, kpe_ref, kc_ref, kp_ref, qlat_ref, qpe_ref,
                   rq_ref, rk_ref, rv_ref, ga_ref, gr_ref):
    D = D_MODEL
    x = x_ref[0]
    sh1 = mod_ref[0, :, 0:D]
    sc1 = mod_ref[0, :, D:2 * D]
    h = (x * (1.0 + sc1) + sh1).astype(BF16)

    yh = _dot(h, wh_ref[...])
    pq = yh[:, 0:Q_LORA]
    pkv = yh[:, Q_LORA:Q_LORA + KV_LORA]
    pkr = yh[:, Q_LORA + KV_LORA:Q_LORA + KV_LORA + LANES]
    qn = (pq * lax.rsqrt(jnp.mean(pq * pq, axis=-1, keepdims=True) + RMS_EPS) * gq_ref[...]).astype(BF16)
    ckv = pkv * lax.rsqrt(jnp.mean(pkv * pkv, axis=-1, keepdims=True) + RMS_EPS) * gkv_ref[...]
    ckv_ref[0] = ckv
    kc_ref[0] = ckv.astype(BF16)
    cosa = cosa_ref[...]
    sina = sina_ref[...]
    kpe = _rope(pkr, cosa, sina, QK_ROPE // 2)[:, 0:QK_ROPE]
    kpe_ref[0] = kpe
    kp_ref[0] = kpe.astype(BF16)

    qpe = _dot(qn, wqp_ref[...])
    for j in range(MLA_HEADS * QK_ROPE // LANES):
        blk = _rope(qpe[:, j * LANES:(j + 1) * LANES], cosa, sina, QK_ROPE // 2) * Q_SCALE
        for hh in range(LANES // QK_ROPE):
            head = j * (LANES // QK_ROPE) + hh
            qpe_ref[0, head] = blk[:, hh * QK_ROPE:(hh + 1) * QK_ROPE].astype(BF16)
    qnope = _dot(qn, wqn_ref[...])
    for head in range(MLA_HEADS):
        qh = qnope[:, head * QK_NOPE:(head + 1) * QK_NOPE].astype(BF16)
        qlat_ref[0, head] = (_dot(qh, wuk_ref[head]) * Q_SCALE).astype(BF16)

    cosb = cosb_ref[...]
    sinb = sinb_ref[...]
    rq = _dot(h, wr_ref[:, 0:D])
    rk = _dot(h, wr_ref[:, D:2 * D])
    for head in range(RET_HEADS):
        sl = slice(head * RET_DK, (head + 1) * RET_DK)
        rq_ref[0, :, sl] = _rope(rq[:, sl], cosb, sinb, RET_DK // 2).astype(rq_ref.dtype)
        rk_ref[0, :, sl] = (_rope(rk[:, sl], cosb, sinb, RET_DK // 2) * (RET_DK ** -0.5)).astype(rk_ref.dtype)
    rv_ref[0] = _dot(h, wr_ref[:, 2 * D:3 * D]).astype(rv_ref.dtype)
    rg = _dot(h, wr_ref[:, 3 * D:4 * D])
    ga_ref[0] = jax.nn.sigmoid(_dot(h, wr_ref[:, 4 * D:5 * D])).astype(BF16)
    gr = jax.nn.sigmoid(_dot(h, wr_ref[:, 5 * D:6 * D]))
    gr_ref[0] = (gr * (rg * jax.nn.sigmoid(rg))).astype(BF16)


def _inproj(x3, mod3, tabs, w, *, tm, act_dtype):
    G, Sg, D = x3.shape
    R = mod3.shape[1]
    assert Sg % tm == 0 and R in (1, tm) and (R == 1 or Sg == tm)
    row = lambda g, i: (g, i, 0)
    head_row = lambda g, i: (g, 0, i, 0)
    tab = pl.BlockSpec((tm, LANES), lambda g, i: (i, 0))
    in_specs = [
        pl.BlockSpec((1, tm, D), row),
        pl.BlockSpec((1, R, 2 * D), lambda g, i: (g, 0, 0)),
        tab, tab, tab, tab,
        _resident(w["w_head"].shape), _resident(w["w_r"].shape),
        _resident(w["g_q"].shape), _resident(w["g_kv"].shape),
        _resident(w["w_uq_nope"].shape), _resident(w["w_uq_pe"].shape), _resident(w["w_ukT"].shape),
    ]
    out_shape = [
        jax.ShapeDtypeStruct((G, Sg, KV_LORA), F32), jax.ShapeDtypeStruct((G, Sg, QK_ROPE), F32),
        jax.ShapeDtypeStruct((G, Sg, KV_LORA), BF16), jax.ShapeDtypeStruct((G, Sg, QK_ROPE), BF16),
        jax.ShapeDtypeStruct((G, MLA_HEADS, Sg, KV_LORA), BF16),
        jax.ShapeDtypeStruct((G, MLA_HEADS, Sg, QK_ROPE), BF16),
        jax.ShapeDtypeStruct((G, Sg, D), act_dtype), jax.ShapeDtypeStruct((G, Sg, D), act_dtype),
        jax.ShapeDtypeStruct((G, Sg, D), act_dtype),
        jax.ShapeDtypeStruct((G, Sg, D), BF16), jax.ShapeDtypeStruct((G, Sg, D), BF16),
    ]
    out_specs = [
        pl.BlockSpec((1, tm, KV_LORA), row), pl.BlockSpec((1, tm, QK_ROPE), row),
        pl.BlockSpec((1, tm, KV_LORA), row), pl.BlockSpec((1, tm, QK_ROPE), row),
        pl.BlockSpec((1, MLA_HEADS, tm, KV_LORA), head_row),
        pl.BlockSpec((1, MLA_HEADS, tm, QK_ROPE), head_row),
        pl.BlockSpec((1, tm, D), row), pl.BlockSpec((1, tm, D), row), pl.BlockSpec((1, tm, D), row),
        pl.BlockSpec((1, tm, D), row), pl.BlockSpec((1, tm, D), row),
    ]
    return pl.pallas_call(
        _inproj_kernel,
        grid=(G, Sg // tm),
        in_specs=in_specs, out_specs=out_specs, out_shape=out_shape,
        compiler_params=_params(("parallel", "parallel")),
        name="inproj",
    )(x3, mod3, *tabs, w["w_head"], w["w_r"], w["g_q"], w["g_kv"],
      w["w_uq_nope"], w["w_uq_pe"], w["w_ukT"])


def _mla_prompt_kernel(qlat_ref, qpe_ref, kc_ref, kp_ref, wuv_ref, o_ref, m_ref, l_ref, acc_ref, *, tq):
    qi = pl.program_id(1)
    m_ref[...] = jnp.full(m_ref.shape, -jnp.inf, F32)
    l_ref[...] = jnp.zeros(l_ref.shape, F32)
    acc_ref[...] = jnp.zeros(acc_ref.shape, F32)

    def tile(j, masked):
        start = pl.multiple_of(j * tq, tq)
        kc = kc_ref[0, pl.ds(start, tq), :]
        kp = kp_ref[0, pl.ds(start, tq), :]
        if masked:
            keep = (lax.broadcasted_iota(jnp.int32, (tq, tq), 1)
                    <= lax.broadcasted_iota(jnp.int32, (tq, tq), 0))
        def scores(head):
            return _dot_nt(qlat_ref[0, head], kc) + _dot_nt(qpe_ref[0, head], kp)

        pending = [scores(hd) for hd in range(SCORE_LOOKAHEAD)]
        for head in range(MLA_HEADS):
            s = pending.pop(0)
            if head + SCORE_LOOKAHEAD < MLA_HEADS:
                pending.append(scores(head + SCORE_LOOKAHEAD))
            if masked:
                s = jnp.where(keep, s, NEG)
            m_prev = m_ref[head]
            m_new = jnp.maximum(m_prev, jnp.max(s, axis=-1, keepdims=True))
            a = jnp.exp2(m_prev - m_new)
            p = jnp.exp2(s - jnp.concatenate([m_new] * (tq // LANES), axis=1))
            psum = p[:, 0:LANES]
            for c in range(1, tq // LANES):
                psum = psum + p[:, c * LANES:(c + 1) * LANES]
            l_ref[head] = a * l_ref[head] + psum
            acc_ref[head] = (jnp.concatenate([a] * (KV_LORA // LANES), axis=1) * acc_ref[head]
                             + _dot(p.astype(BF16), kc))
            m_ref[head] = m_new

    def body(j, carry):
        tile(j, False)
        return carry

    lax.fori_loop(0, qi, body, 0)
    tile(qi, True)
    for head in range(MLA_HEADS):
        inv = 1.0 / jnp.sum(l_ref[head], axis=-1, keepdims=True)
        oh = (acc_ref[head] * inv).astype(BF16)
        o_ref[0, :, head * V_HEAD:(head + 1) * V_HEAD] = _dot(oh, wuv_ref[head]).astype(BF16)


def _mla_prompt(qlat, qpe, kc, kp, w_uv, *, tq):
    B, Hh, S, _ = qlat.shape
    assert tq % LANES == 0
    return pl.pallas_call(
        functools.partial(_mla_prompt_kernel, tq=tq),
        grid=(B, S // tq),
        in_specs=[pl.BlockSpec((1, Hh, tq, KV_LORA), lambda b, i: (b, 0, i, 0)),
                  pl.BlockSpec((1, Hh, tq, QK_ROPE), lambda b, i: (b, 0, i, 0)),
                  pl.BlockSpec((1, S, KV_LORA), lambda b, i: (b, 0, 0)),
                  pl.BlockSpec((1, S, QK_ROPE), lambda b, i: (b, 0, 0)),
                  _resident(w_uv.shape)],
        out_specs=pl.BlockSpec((1, tq, D_MODEL), lambda b, i: (b, i, 0)),
        out_shape=jax.ShapeDtypeStruct((B, S, D_MODEL), BF16),
        scratch_shapes=[pltpu.VMEM((Hh, tq, LANES), F32), pltpu.VMEM((Hh, tq, LANES), F32),
                        pltpu.VMEM((Hh, tq, KV_LORA), F32)],
        compiler_params=_params(("parallel", "parallel")),
        name="mla_prompt",
    )(qlat, qpe, kc, kp, w_uv)


def _mla_sample_kernel(pt_ref, q1_ref, q2_ref, cn_ref, kn_ref, *refs, pages, tokens):
    del pt_ref
    ck_refs = refs[:pages]
    kp_refs = refs[pages:2 * pages]
    o_ref, m_ref, l_ref, acc_ref = refs[2 * pages:]
    g = pl.program_id(1)

    @pl.when(g == 0)
    def _():
        m_ref[...] = jnp.full(m_ref.shape, -jnp.inf, F32)
        l_ref[...] = jnp.zeros(l_ref.shape, F32)
        acc_ref[...] = jnp.zeros(acc_ref.shape, F32)

    q1 = q1_ref[0]
    q2 = q2_ref[0]
    kbs = [ck_refs[j][...].astype(BF16) for j in range(pages)]
    s = jnp.concatenate(
        [_dot_nt(q1, kbs[j]) + _dot(q2, kp_refs[j][...].astype(BF16)) for j in range(pages)], axis=1)
    m_prev = m_ref[...]
    m_new = jnp.maximum(m_prev, jnp.max(s, axis=-1, keepdims=True))
    a = jnp.exp2(m_prev - m_new)
    p = jnp.exp2(s - m_new)
    l_ref[...] = a * l_ref[...] + jnp.sum(p, axis=-1, keepdims=True)
    pv = _dot(p[:, 0:PAGE_SIZE].astype(BF16), kbs[0])
    for j in range(1, pages):
        pv = pv + _dot(p[:, j * PAGE_SIZE:(j + 1) * PAGE_SIZE].astype(BF16), kbs[j])
    acc_ref[...] = a * acc_ref[...] + pv
    m_ref[...] = m_new

    @pl.when(g == pl.num_programs(1) - 1)
    def _():
        q1f = q1.astype(F32)
        q2f = q2.astype(F32)
        cn = cn_ref[0]
        kn = kn_ref[0]
        tok = lax.broadcasted_iota(jnp.int32, m_ref.shape, 0) & (tokens - 1)
        cols = []
        for t in range(tokens):
            sc = (jnp.sum(q1f * cn[t:t + 1, :], axis=-1, keepdims=True)
                  + jnp.sum(q2f * kn[t:t + 1, :], axis=-1, keepdims=True))
            cols.append(jnp.where(tok >= t, sc, NEG))
        m_old = m_ref[...]
        m_fin = m_old
        for sc in cols:
            m_fin = jnp.maximum(m_fin, sc)
        a2 = jnp.exp2(m_old - m_fin)
        l_fin = a2 * l_ref[...]
        acc = a2 * acc_ref[...]
        for t, sc in enumerate(cols):
            pt = jnp.exp2(sc - m_fin)
            l_fin = l_fin + pt
            acc = acc + pt * cn[t:t + 1, :]
        o_ref[0] = acc * (1.0 / l_fin)


def _mla_sample(page_table, q1, q2, cnew, knew, cache_ckv, cache_kpe_t, *, pages):
    DB, n_pages = page_table.shape
    rows = q1.shape[1]
    tokens = cnew.shape[1]
    assert n_pages % pages == 0 and tokens & (tokens - 1) == 0

    def page_spec(rows_, cols_, j):
        return pl.BlockSpec((None, None, rows_, cols_),
                            lambda b, g, pt: (0, pt[b, g * pages + j], 0, 0))

    per_batch = lambda b, g, pt: (b, 0, 0)
    grid_spec = pltpu.PrefetchScalarGridSpec(
        num_scalar_prefetch=1,
        grid=(DB, n_pages // pages),
        in_specs=[pl.BlockSpec((1, rows, KV_LORA), per_batch),
                  pl.BlockSpec((1, rows, QK_ROPE), per_batch),
                  pl.BlockSpec((1, tokens, KV_LORA), per_batch),
                  pl.BlockSpec((1, tokens, QK_ROPE), per_batch)]
                 + [page_spec(PAGE_SIZE, KV_LORA, j) for j in range(pages)]
                 + [page_spec(QK_ROPE, PAGE_SIZE, j) for j in range(pages)],
        out_specs=pl.BlockSpec((1, rows, KV_LORA), per_batch),
        scratch_shapes=[pltpu.VMEM((rows, 1), F32), pltpu.VMEM((rows, 1), F32),
                        pltpu.VMEM((rows, KV_LORA), F32)],
    )
    return pl.pallas_call(
        functools.partial(_mla_sample_kernel, pages=pages, tokens=tokens),
        grid_spec=grid_spec,
        out_shape=jax.ShapeDtypeStruct((DB, rows, KV_LORA), F32),
        compiler_params=_params(("parallel", "arbitrary")),
        name="mla_sample",
    )(page_table, q1, q2, cnew, knew, *([cache_ckv] * pages), *([cache_kpe_t] * pages))


def _retention_kernel(*refs, T, L, has_s0):
    C = RET_CHUNK
    padded = T < C
    if has_s0:
        q_ref, k_ref, v_ref, s0_ref, y_ref, s_ref = refs[:6]
        rest = refs[6:]
    else:
        q_ref, k_ref, v_ref, y_ref, s_ref = refs[:5]
        rest = refs[5:]
    heads = range(RET_HEADS)
    ri = lax.broadcasted_iota(jnp.int32, (C, C), 0).astype(F32)
    ci = lax.broadcasted_iota(jnp.int32, (C, C), 1).astype(F32)
    diff = ri - ci
    causal = diff >= 0.0
    log_gamma = [math.log(1.0 - 2.0 ** (-5.0 - h)) for h in heads]
    dmask = [jnp.where(causal, jnp.exp(jnp.where(causal, diff, 0.0) * lg), 0.0) for lg in log_gamma]
    q_decay = [jnp.exp((ri + 1.0) * lg) for lg in log_gamma]
    k_decay = [jnp.exp((L - 1.0 - ri) * lg) for lg in log_gamma]
    chunk_decay = [math.exp(L * lg) for lg in log_gamma]

    def col(h):
        return slice(h * RET_DK, (h + 1) * RET_DK)

    def chunk(load, store):
        q = [load(q_ref if not padded else rest[0], h).astype(BF16) for h in heads]
        k = [load(k_ref if not padded else rest[1], h) for h in heads]
        v = [load(v_ref if not padded else rest[2], h).astype(BF16) for h in heads]
        s = [s_ref[0, h] for h in heads]
        scores = [_dot_nt(q[h], k[h].astype(BF16)) for h in heads]
        cross = [_dot(q[h], s[h].astype(BF16)) for h in heads]
        kd_t = [(k[h].astype(F32) * k_decay[h]).T.astype(BF16) for h in heads]
        inner = [_dot((scores[h] * dmask[h]).astype(BF16), v[h]) for h in heads]
        upd = [_dot(kd_t[h], v[h]) for h in heads]
        for h in heads:
            y = inner[h] + cross[h] * q_decay[h]
            mu = jnp.mean(y, axis=-1, keepdims=True)
            yc = y - mu
            var = jnp.mean(yc * yc, axis=-1, keepdims=True)
            store(h, yc * lax.rsqrt(var + LN_EPS))
            s_ref[0, h] = chunk_decay[h] * s[h] + upd[h]

    for h in heads:
        s_ref[0, h] = s0_ref[0, h] if has_s0 else jnp.zeros((RET_DK, RET_DV), F32)
    if padded:
        for src, dst in zip((q_ref, k_ref, v_ref), rest):
            dst[...] = jnp.zeros(dst.shape, F32)
            dst[0:T, :] = src[0].astype(F32)

        def store(h, yn):
            y_ref[0, :, col(h)] = yn[0:T, :].astype(y_ref.dtype)
        chunk(lambda ref, h: ref[:, col(h)], store)
    else:
        def body(c, carry):
            rows = pl.ds(pl.multiple_of(c * C, C), C)

            def store(h, yn):
                y_ref[0, rows, col(h)] = yn.astype(y_ref.dtype)
            chunk(lambda ref, h: ref[0, rows, col(h)], store)
            return carry
        lax.fori_loop(0, T // C, body, 0)


def _retention(rq, rk, rv, s0, *, L, out_dtype):
    B, T, D = rq.shape
    has_s0 = s0 is not None
    padded = T < RET_CHUNK
    assert padded or T % RET_CHUNK == 0
    tok = pl.BlockSpec((1, T, D), lambda b: (b, 0, 0))
    state = pl.BlockSpec((1, RET_HEADS, RET_DK, RET_DV), lambda b: (b, 0, 0, 0))
    in_specs = [tok, tok, tok] + ([state] if has_s0 else [])
    args = (rq, rk, rv) + ((s0,) if has_s0 else ())
    return pl.pallas_call(
        functools.partial(_retention_kernel, T=T, L=L, has_s0=has_s0),
        grid=(B,),
        in_specs=in_specs,
        out_specs=[tok, state],
        out_shape=[jax.ShapeDtypeStruct((B, T, D), out_dtype),
                   jax.ShapeDtypeStruct((B, RET_HEADS, RET_DK, RET_DV), F32)],
        scratch_shapes=[pltpu.VMEM((RET_CHUNK, D), F32)] * 3 if padded else [],
        compiler_params=_params(("parallel",)),
        name="retention",
    )(*args)


def _post_kernel(*refs, apply_uv):
    if apply_uv:
        (x_ref, att_ref, y_ref, ga_ref, gr_ref, mod_ref, wuv_ref, wo_ref, wup_ref, wdn_ref,
         g1_ref, b1_ref, g2_ref, b2_ref, o_ref) = refs
    else:
        (x_ref, att_ref, y_ref, ga_ref, gr_ref, mod_ref, wo_ref, wup_ref, wdn_ref,
         g1_ref, b1_ref, g2_ref, b2_ref, o_ref) = refs
    D = D_MODEL
    x = x_ref[0]
    if apply_uv:
        o_a = jnp.concatenate([_dot(att_ref[0, head].astype(BF16), wuv_ref[head])
                               for head in range(MLA_HEADS)], axis=1)
    else:
        o_a = att_ref[0].astype(F32)
    mixv = ga_ref[0].astype(F32) * o_a + gr_ref[0].astype(F32) * y_ref[0].astype(F32)
    mix = _dot(mixv.astype(BF16), wo_ref[...])
    g1 = mod_ref[0, :, 2 * D:3 * D]
    sh2 = mod_ref[0, :, 3 * D:4 * D]
    sc2 = mod_ref[0, :, 4 * D:5 * D]
    g2 = mod_ref[0, :, 5 * D:6 * D]
    x1 = _layer_norm(ALPHA * x + (1.0 + g1) * mix, g1_ref[...], b1_ref[...])
    h2 = (x1 * (1.0 + sc2) + sh2).astype(BF16)
    m = None
    for c in range(D_FF // D):
        u = jnp.maximum(_dot(h2, wup_ref[:, c * D:(c + 1) * D]), 0.0)
        part = _dot((u * u).astype(BF16), wdn_ref[c * D:(c + 1) * D, :])
        m = part if m is None else m + part
    o_ref[0] = _layer_norm(ALPHA * x1 + (1.0 + g2) * m, g2_ref[...], b2_ref[...])


def _post(x3, att, y3, ga, gr, mod3, w, *, tm, apply_uv):
    G, Sg, D = x3.shape
    R = mod3.shape[1]
    row = lambda g, i: (g, i, 0)
    act = pl.BlockSpec((1, tm, D), row)
    if apply_uv:
        att_spec = pl.BlockSpec((1, MLA_HEADS, tm, KV_LORA), lambda g, i: (g, 0, i, 0))
    else:
        att_spec = act
    in_specs = [act, att_spec, act, act, act, pl.BlockSpec((1, R, 6 * D), lambda g, i: (g, 0, 0))]
    args = [x3, att, y3, ga, gr, mod3]
    names = (["w_uv"] if apply_uv else []) + ["w_o", "w_up", "w_down", "ln1_g", "ln1_b", "ln2_g", "ln2_b"]
    for n in names:
        in_specs.append(_resident(w[n].shape))
        args.append(w[n])
    return pl.pallas_call(
        functools.partial(_post_kernel, apply_uv=apply_uv),
        grid=(G, Sg // tm),
        in_specs=in_specs,
        out_specs=act,
        out_shape=jax.ShapeDtypeStruct((G, Sg, D), F32),
        compiler_params=_params(("parallel", "parallel")),
        name="post",
    )(*args)


def _rope_tables(pos, dim):
    half = dim // 2
    inv = ROPE_BASE ** (-jnp.arange(half, dtype=F32) / half)
    ang = pos.astype(F32)[:, None] * inv[None, :]
    cos, sin = jnp.cos(ang), jnp.sin(ang)
    reps = LANES // dim
    return (jnp.tile(jnp.concatenate([cos, cos], -1), (1, reps)),
            jnp.tile(jnp.concatenate([-sin, sin], -1), (1, reps)))


def kernel(x_prompt, x_sample, c_prompt, c_sample, cache_ckv, cache_kpe, state_ret, page_table,
           w_ada, b_ada, w_in, g_qnorm, g_kvnorm, w_uq, w_uk, w_uv, w_o,
           ln1_g, ln1_b, w_up, w_down, ln2_g, ln2_b):
    B, S, D = x_prompt.shape
    DB, T, _ = x_sample.shape
    n_pages = page_table.shape[1]
    past_len = n_pages * PAGE_SIZE
    assert DEPTH == 1 and w_in.shape[0] == 1
    l = 0
    n_head = Q_LORA + KV_LORA + QK_ROPE

    w_kr = w_in[l][:, Q_LORA + KV_LORA:n_head]
    w = {
        "w_head": jnp.concatenate([w_in[l][:, :n_head], w_kr], axis=1).astype(BF16),
        "w_r": w_in[l][:, n_head:].astype(BF16),
        "g_q": g_qnorm[l][None, :], "g_kv": g_kvnorm[l][None, :],
        "w_uq_nope": w_uq[l][:, :, :QK_NOPE].reshape(Q_LORA, MLA_HEADS * QK_NOPE).astype(BF16),
        "w_uq_pe": w_uq[l][:, :, QK_NOPE:].reshape(Q_LORA, MLA_HEADS * QK_ROPE).astype(BF16),
        "w_ukT": jnp.transpose(w_uk[l], (1, 2, 0)).astype(BF16),
        "w_uv": jnp.transpose(w_uv[l], (1, 0, 2)).astype(BF16),
        "w_o": w_o[l].astype(BF16), "w_up": w_up[l].astype(BF16), "w_down": w_down[l].astype(BF16),
        "ln1_g": ln1_g[l][None, :], "ln1_b": ln1_b[l][None, :],
        "ln2_g": ln2_g[l][None, :], "ln2_b": ln2_b[l][None, :],
    }

    n_c = B + DB
    pad = (-n_c) % 16
    c_all = jnp.concatenate([c_prompt, c_sample, jnp.zeros((pad, D), F32)], axis=0)
    ada = _ada(c_all, w_ada[l], b_ada[l][None, :])
    mod_p = ada[:B].reshape(B, 1, 6 * D)
    mod_s = jnp.repeat(ada[B:n_c], T, axis=0).reshape(1, DB * T, 6 * D)

    pos_p = jnp.arange(S)
    tabs_p = _rope_tables(pos_p, QK_ROPE) + _rope_tables(pos_p, RET_DK)
    (ckv_p, kpe_p, kc_p, kp_p, qlat_p, qpe_p, rq_p, rk_p, rv_p, ga_p, gr_p) = _inproj(
        x_prompt, mod_p, tabs_p, w, tm=256, act_dtype=BF16)
    oa_p = _mla_prompt(qlat_p, qpe_p, kc_p, kp_p, w["w_uv"], tq=256)
    y_p, ret_p = _retention(rq_p, rk_p, rv_p, None, L=min(RET_CHUNK, S), out_dtype=BF16)
    out_p = _post(x_prompt, oa_p, y_p, ga_p, gr_p, mod_p, w, tm=256, apply_uv=False)

    n_s = DB * T
    pos_s = jnp.tile(past_len + jnp.arange(T), DB)
    tabs_s = _rope_tables(pos_s, QK_ROPE) + _rope_tables(pos_s, RET_DK)
    xs = x_sample.reshape(1, n_s, D)
    (ckv_s, kpe_s, _, _, qlat_s, qpe_s, rq_s, rk_s, rv_s, ga_s, gr_s) = _inproj(
        xs, mod_s, tabs_s, w, tm=n_s, act_dtype=F32)

    def to_batch_rows(q):
        d = q.shape[-1]
        return q[0].reshape(MLA_HEADS, DB, T, d).transpose(1, 0, 2, 3).reshape(DB, MLA_HEADS * T, d)

    olat_s = _mla_sample(page_table, to_batch_rows(qlat_s), to_batch_rows(qpe_s),
                         ckv_s.reshape(DB, T, KV_LORA), kpe_s.reshape(DB, T, QK_ROPE),
                         cache_ckv, jnp.swapaxes(cache_kpe, 2, 3), pages=16)
    olat_s = olat_s.reshape(DB, MLA_HEADS, T, KV_LORA).transpose(1, 0, 2, 3).reshape(1, MLA_HEADS, n_s, KV_LORA)
    y_s, ret_s = _retention(rq_s.reshape(DB, T, D), rk_s.reshape(DB, T, D), rv_s.reshape(DB, T, D),
                            state_ret[l], L=T, out_dtype=F32)
    out_s = _post(xs, olat_s, y_s.reshape(1, n_s, D), ga_s, gr_s, mod_s, w, tm=n_s, apply_uv=True)

    return (out_p, out_s.reshape(DB, T, D),
            ckv_p[None], kpe_p[None], ret_p[None],
            ckv_s.reshape(1, DB, T, KV_LORA), kpe_s.reshape(1, DB, T, QK_ROPE), ret_s[None])
```

```python
import functools
import math

import jax
import jax.numpy as jnp
from jax import lax
from jax.experimental import pallas as pl
from jax.experimental.pallas import tpu as pltpu

D_MODEL = 1024
DEPTH = 1
PAGE_SIZE = 128
MLA_HEADS = 8
Q_LORA = 384
KV_LORA = 256
QK_NOPE = 128
QK_ROPE = 64
V_HEAD = D_MODEL // MLA_HEADS
MLA_SCALE = (QK_NOPE + QK_ROPE) ** -0.5
LOG2E = 1.4426950408889634
Q_SCALE = MLA_SCALE * LOG2E
RET_HEADS = 8
RET_DK = 128
RET_DV = D_MODEL // RET_HEADS
RET_CHUNK = 128
D_FF = 4 * D_MODEL
ROPE_BASE = 10000.0
LN_EPS = 1e-5
RMS_EPS = 1e-6
NEG = -1e30
ALPHA = (2.0 * DEPTH) ** 0.25

SCORE_LOOKAHEAD = 3
SAMPLE_SLOTS = 4
LANES = 128
V7X_VMEM_BYTES = 64 * 1024 * 1024
VMEM_LIMIT = V7X_VMEM_BYTES - 8 * 1024 * 1024

F32 = jnp.float32
BF16 = jnp.bfloat16


def _params(sem):
    return pltpu.CompilerParams(dimension_semantics=sem, vmem_limit_bytes=VMEM_LIMIT)


def _resident(shape):
    nd = len(shape)
    return pl.BlockSpec(shape, lambda *_: (0,) * nd, pipeline_mode=pl.Buffered(1))


def _dot(a, b):
    return jnp.dot(a, b, preferred_element_type=F32)


def _dot_nt(a, b):
    return lax.dot_general(a, b, (((1,), (1,)), ((), ())), preferred_element_type=F32)


def _rope(x, cos, sin_signed, half):
    lanes = x.shape[-1]
    if 2 * half == lanes:
        partner = pltpu.roll(x, half, axis=1)
    else:
        lane = lax.broadcasted_iota(jnp.int32, x.shape, 1)
        first = (lane & (2 * half - 1)) < half
        partner = jnp.where(first, pltpu.roll(x, lanes - half, axis=1), pltpu.roll(x, half, axis=1))
    return x * cos + partner * sin_signed


def _layer_norm(v, g, b):
    mu = jnp.mean(v, axis=-1, keepdims=True)
    vc = v - mu
    var = jnp.mean(vc * vc, axis=-1, keepdims=True)
    return vc * lax.rsqrt(var + LN_EPS) * g + b


def _ada_kernel(c_ref, w_ref, b_ref, o_ref):
    c = c_ref[...]
    s = (c * jax.nn.sigmoid(c)).astype(BF16)
    o_ref[...] = _dot(s, w_ref[...].astype(BF16)) + b_ref[...]


def _ada(c_all, w_ada, b_ada):
    rows = c_all.shape[0]
    tn = D_MODEL
    return pl.pallas_call(
        _ada_kernel,
        grid=(6 * D_MODEL // tn,),
        in_specs=[pl.BlockSpec((rows, D_MODEL), lambda j: (0, 0)),
                  pl.BlockSpec((D_MODEL, tn), lambda j: (0, j)),
                  pl.BlockSpec((1, tn), lambda j: (0, j))],
        out_specs=pl.BlockSpec((rows, tn), lambda j: (0, j)),
        out_shape=jax.ShapeDtypeStruct((rows, 6 * D_MODEL), F32),
        compiler_params=_params(("parallel",)),
        name="ada",
    )(c_all, w_ada, b_ada)


def _inproj_kernel(x_ref, mod_ref, cosa_ref, sina_ref, cosb_ref, sinb_ref,
                   wh_ref, wr_ref, gq_ref, gkv_ref, wqn_ref, wqp_ref, wuk_ref,
                   ckv_ref, kpe_ref, kc_ref, kp_ref, qlat_ref, qpe_ref,
                   rq_ref, rk_ref, rv_ref, ga_ref, gr_ref):
    D = D_MODEL
    x = x_ref[0]
    sh1 = mod_ref[0, :, 0:D]
    sc1 = mod_ref[0, :, D:2 * D]
    h = (x * (1.0 + sc1) + sh1).astype(BF16)

    yh = _dot(h, wh_ref[...])
    pq = yh[:, 0:Q_LORA]
    pkv = yh[:, Q_LORA:Q_LORA + KV_LORA]
    pkr = yh[:, Q_LORA + KV_LORA:Q_LORA + KV_LORA + LANES]
    qn = (pq * lax.rsqrt(jnp.mean(pq * pq, axis=-1, keepdims=True) + RMS_EPS) * gq_ref[...]).astype(BF16)
    ckv = pkv * lax.rsqrt(jnp.mean(pkv * pkv, axis=-1, keepdims=True) + RMS_EPS) * gkv_ref[...]
    ckv_ref[0] = ckv
    kc_ref[0] = ckv.astype(BF16)
    cosa = cosa_ref[...]
    sina = sina_ref[...]
    kpe = _rope(pkr, cosa, sina, QK_ROPE // 2)[:, 0:QK_ROPE]
    kpe_ref[0] = kpe
    kp_ref[0] = kpe.astype(BF16)

    qpe = _dot(qn, wqp_ref[...])
    for j in range(MLA_HEADS * QK_ROPE // LANES):
        blk = _rope(qpe[:, j * LANES:(j + 1) * LANES], cosa, sina, QK_ROPE // 2) * Q_SCALE
        for hh in range(LANES // QK_ROPE):
            head = j * (LANES // QK_ROPE) + hh
            qpe_ref[0, head] = blk[:, hh * QK_ROPE:(hh + 1) * QK_ROPE].astype(BF16)
    qnope = _dot(qn, wqn_ref[...])
    for head in range(MLA_HEADS):
        qh = qnope[:, head * QK_NOPE:(head + 1) * QK_NOPE].astype(BF16)
        qlat_ref[0, head] = (_dot(qh, wuk_ref[head]) * Q_SCALE).astype(BF16)

    cosb = cosb_ref[...]
    sinb = sinb_ref[...]
    rq = _dot(h, wr_ref[:, 0:D])
    rk = _dot(h, wr_ref[:, D:2 * D])
    for head in range(RET_HEADS):
        sl = slice(head * RET_DK, (head + 1) * RET_DK)
        rq_ref[0, :, sl] = _rope(rq[:, sl], cosb, sinb, RET_DK // 2).astype(rq_ref.dtype)
        rk_ref[0, :, sl] = (_rope(rk[:, sl], cosb, sinb, RET_DK // 2) * (RET_DK ** -0.5)).astype(rk_ref.dtype)
    rv_ref[0] = _dot(h, wr_ref[:, 2 * D:3 * D]).astype(rv_ref.dtype)
    rg = _dot(h, wr_ref[:, 3 * D:4 * D])
    ga_ref[0] = jax.nn.sigmoid(_dot(h, wr_ref[:, 4 * D:5 * D])).astype(BF16)
    gr = jax.nn.sigmoid(_dot(h, wr_ref[:, 5 * D:6 * D]))
    gr_ref[0] = (gr * (rg * jax.nn.sigmoid(rg))).astype(BF16)


def _inproj(x3, mod3, tabs, w, *, tm, act_dtype):
    G, Sg, D = x3.shape
    R = mod3.shape[1]
    assert Sg % tm == 0 and R in (1, tm) and (R == 1 or Sg == tm)
    row = lambda g, i: (g, i, 0)
    head_row = lambda g, i: (g, 0, i, 0)
    tab = pl.BlockSpec((tm, LANES), lambda g, i: (i, 0))
    in_specs = [
        pl.BlockSpec((1, tm, D), row),
        pl.BlockSpec((1, R, 2 * D), lambda g, i: (g, 0, 0)),
        tab, tab, tab, tab,
        _resident(w["w_head"].shape), _resident(w["w_r"].shape),
        _resident(w["g_q"].shape), _resident(w["g_kv"].shape),
        _resident(w["w_uq_nope"].shape), _resident(w["w_uq_pe"].shape), _resident(w["w_ukT"].shape),
    ]
    out_shape = [
        jax.ShapeDtypeStruct((G, Sg, KV_LORA), F32), jax.ShapeDtypeStruct((G, Sg, QK_ROPE), F32),
        jax.ShapeDtypeStruct((G, Sg, KV_LORA), BF16), jax.ShapeDtypeStruct((G, Sg, QK_ROPE), BF16),
        jax.ShapeDtypeStruct((G, MLA_HEADS, Sg, KV_LORA), BF16),
        jax.ShapeDtypeStruct((G, MLA_HEADS, Sg, QK_ROPE), BF16),
        jax.ShapeDtypeStruct((G, Sg, D), act_dtype), jax.ShapeDtypeStruct((G, Sg, D), act_dtype),
        jax.ShapeDtypeStruct((G, Sg, D), act_dtype),
        jax.ShapeDtypeStruct((G, Sg, D), BF16), jax.ShapeDtypeStruct((G, Sg, D), BF16),
    ]
    out_specs = [
        pl.BlockSpec((1, tm, KV_LORA), row), pl.BlockSpec((1, tm, QK_ROPE), row),
        pl.BlockSpec((1, tm, KV_LORA), row), pl.BlockSpec((1, tm, QK_ROPE), row),
        pl.BlockSpec((1, MLA_HEADS, tm, KV_LORA), head_row),
        pl.BlockSpec((1, MLA_HEADS, tm, QK_ROPE), head_row),
        pl.BlockSpec((1, tm, D), row), pl.BlockSpec((1, tm, D), row), pl.BlockSpec((1, tm, D), row),
        pl.BlockSpec((1, tm, D), row), pl.BlockSpec((1, tm, D), row),
    ]
    return pl.pallas_call(
        _inproj_kernel,
        grid=(G, Sg // tm),
        in_specs=in_specs, out_specs=out_specs, out_shape=out_shape,
        compiler_params=_params(("parallel", "parallel")),
        name="inproj",
    )(x3, mod3, *tabs, w["w_head"], w["w_r"], w["g_q"], w["g_kv"],
      w["w_uq_nope"], w["w_uq_pe"], w["w_ukT"])


def _mla_prompt_kernel(qlat_ref, qpe_ref, kc_ref, kp_ref, wuv_ref, o_ref,
                       m_ref, l_ref, acc_ref, s_ref, *, tq):
    qi = pl.program_id(1)
    m_ref[...] = jnp.full(m_ref.shape, -jnp.inf, F32)
    l_ref[...] = jnp.zeros(l_ref.shape, F32)
    acc_ref[...] = jnp.zeros(acc_ref.shape, F32)

    def kv(j):
        start = pl.multiple_of(j * tq, tq)
        return kc_ref[0, pl.ds(start, tq), :], kp_ref[0, pl.ds(start, tq), :]

    def scores(head, kc, kp):
        return _dot_nt(qlat_ref[0, head], kc) + _dot_nt(qpe_ref[0, head], kp)

    kc0, kp0 = kv(0)
    for hd in range(SCORE_LOOKAHEAD):
        s_ref[hd] = scores(hd, kc0, kp0)

    def tile(j, masked):
        kc, kp = kv(j)
        if masked:
            keep = (lax.broadcasted_iota(jnp.int32, (tq, tq), 1)
                    <= lax.broadcasted_iota(jnp.int32, (tq, tq), 0))
        else:
            kc_next, kp_next = kv(j + 1)
        pending = [s_ref[hd] for hd in range(SCORE_LOOKAHEAD)]
        for head in range(MLA_HEADS):
            s = pending.pop(0)
            ahead = head + SCORE_LOOKAHEAD
            if ahead < MLA_HEADS:
                pending.append(scores(ahead, kc, kp))
            elif not masked:
                s_ref[ahead - MLA_HEADS] = scores(ahead - MLA_HEADS, kc_next, kp_next)
            if masked:
                s = jnp.where(keep, s, NEG)
            m_prev = m_ref[head]
            m_new = jnp.maximum(m_prev, jnp.max(s, axis=-1, keepdims=True))
            a = jnp.exp2(m_prev - m_new)
            p = jnp.exp2(s - jnp.concatenate([m_new] * (tq // LANES), axis=1))
            psum = p[:, 0:LANES]
            for c in range(1, tq // LANES):
                psum = psum + p[:, c * LANES:(c + 1) * LANES]
            l_ref[head] = a * l_ref[head] + psum
            acc_ref[head] = (jnp.concatenate([a] * (KV_LORA // LANES), axis=1) * acc_ref[head]
                             + _dot(p.astype(BF16), kc))
            m_ref[head] = m_new

    def body(j, carry):
        tile(j, False)
        return carry

    lax.fori_loop(0, qi, body, 0)
    tile(qi, True)
    for head in range(MLA_HEADS):
        inv = 1.0 / jnp.sum(l_ref[head], axis=-1, keepdims=True)
        oh = (acc_ref[head] * inv).astype(BF16)
        o_ref[0, :, head * V_HEAD:(head + 1) * V_HEAD] = _dot(oh, wuv_ref[head]).astype(BF16)


def _mla_prompt(qlat, qpe, kc, kp, w_uv, *, tq):
    B, Hh, S, _ = qlat.shape
    assert tq % LANES == 0
    return pl.pallas_call(
        functools.partial(_mla_prompt_kernel, tq=tq),
        grid=(B, S // tq),
        in_specs=[pl.BlockSpec((1, Hh, tq, KV_LORA), lambda b, i: (b, 0, i, 0)),
                  pl.BlockSpec((1, Hh, tq, QK_ROPE), lambda b, i: (b, 0, i, 0)),
                  pl.BlockSpec((1, S, KV_LORA), lambda b, i: (b, 0, 0)),
                  pl.BlockSpec((1, S, QK_ROPE), lambda b, i: (b, 0, 0)),
                  _resident(w_uv.shape)],
        out_specs=pl.BlockSpec((1, tq, D_MODEL), lambda b, i: (b, i, 0)),
        out_shape=jax.ShapeDtypeStruct((B, S, D_MODEL), BF16),
        scratch_shapes=[pltpu.VMEM((Hh, tq, LANES), F32), pltpu.VMEM((Hh, tq, LANES), F32),
                        pltpu.VMEM((Hh, tq, KV_LORA), F32), pltpu.VMEM((SCORE_LOOKAHEAD, tq, tq), F32)],
        compiler_params=_params(("parallel", "parallel")),
        name="mla_prompt",
    )(qlat, qpe, kc, kp, w_uv)


def _mla_sample_kernel(pt_ref, q1_ref, q2_ref, cn_ref, kn_ref, ck_hbm, kp_hbm, o_ref,
                       ck_buf, kp_buf, sems, *, pages, groups, tokens):
    b = pl.program_id(0)
    n_b = pl.num_programs(0)
    ahead = SAMPLE_SLOTS - 1
    assert groups % SAMPLE_SLOTS == 0 and ahead <= groups

    def group_copies(bb, g):
        slot = g % SAMPLE_SLOTS
        out = []
        for j in range(pages):
            page = pt_ref[bb, g * pages + j]
            out.append(pltpu.make_async_copy(ck_hbm.at[0, page], ck_buf.at[slot, j], sems.at[0, slot]))
            out.append(pltpu.make_async_copy(kp_hbm.at[0, page], kp_buf.at[slot, j], sems.at[1, slot]))
        return out

    def start_group(bb, g):
        for cp in group_copies(bb, g):
            cp.start()

    @pl.when(b == 0)
    def _():
        for g in range(ahead):
            start_group(b, g)

    q1 = q1_ref[0]
    q2 = q2_ref[0]
    rows = q1.shape[0]

    def scores(g):
        for cp in group_copies(b, g):
            cp.wait()
        slot = g % SAMPLE_SLOTS
        return [_dot_nt(q1, ck_buf[slot, j].astype(BF16)) + _dot(q2, kp_buf[slot, j].astype(BF16))
                for j in range(pages)]

    def update(g, cols, m, l, acc):
        slot = g % SAMPLE_SLOTS
        mx = cols[0]
        for c in cols[1:]:
            mx = jnp.maximum(mx, c)
        m_new = jnp.maximum(m, jnp.max(mx, axis=-1, keepdims=True))
        a = jnp.exp2(m - m_new)
        ps = [jnp.exp2(c - m_new) for c in cols]
        psum = ps[0]
        for p in ps[1:]:
            psum = psum + p
        pv = _dot(ps[0].astype(BF16), ck_buf[slot, 0].astype(BF16))
        for j in range(1, pages):
            pv = pv + _dot(ps[j].astype(BF16), ck_buf[slot, j].astype(BF16))
        acc = jnp.concatenate([a] * (KV_LORA // LANES), axis=1) * acc + pv
        return m_new, a * l + psum, acc

    m = jnp.full((rows, LANES), -jnp.inf, F32)
    l = jnp.zeros((rows, LANES), F32)
    acc = jnp.zeros((rows, KV_LORA), F32)
    pending = scores(0)
    for g in range(groups):
        nxt = scores(g + 1) if g + 1 < groups else None
        m, l, acc = update(g, pending, m, l, acc)
        pending = nxt
        if g + ahead < groups:
            start_group(b, g + ahead)
        else:
            @pl.when(b + 1 < n_b)
            def _(g=g):
                start_group(b + 1, g + ahead - groups)

    q1f = q1.astype(F32)
    q2f = q2.astype(F32)
    cn = cn_ref[0]
    kn = kn_ref[0]
    tok = lax.broadcasted_iota(jnp.int32, (rows, LANES), 0) & (tokens - 1)
    cols = []
    for t in range(tokens):
        sc = (jnp.sum(q1f * cn[t:t + 1, :], axis=-1, keepdims=True)
              + jnp.sum(q2f * kn[t:t + 1, :], axis=-1, keepdims=True))
        cols.append(jnp.where(tok >= t, sc, NEG))
    m_fin = m
    for sc in cols:
        m_fin = jnp.maximum(m_fin, sc)
    a2 = jnp.exp2(m - m_fin)
    l_fin = jnp.sum(a2 * l, axis=-1, keepdims=True)
    acc = jnp.concatenate([a2] * (KV_LORA // LANES), axis=1) * acc
    for t, sc in enumerate(cols):
        pt = jnp.exp2(sc - m_fin)
        l_fin = l_fin + pt[:, 0:1]
        acc = acc + jnp.concatenate([pt] * (KV_LORA // LANES), axis=1) * cn[t:t + 1, :]
    o_ref[0] = acc * (1.0 / l_fin)


def _mla_sample(page_table, q1, q2, cnew, knew, cache_ckv, cache_kpe_t, *, pages):
    DB, n_pages = page_table.shape
    rows = q1.shape[1]
    tokens = cnew.shape[1]
    assert n_pages % pages == 0 and tokens & (tokens - 1) == 0
    groups = n_pages // pages

    per_batch = lambda b, pt: (b, 0, 0)
    grid_spec = pltpu.PrefetchScalarGridSpec(
        num_scalar_prefetch=1,
        grid=(DB,),
        in_specs=[pl.BlockSpec((1, rows, KV_LORA), per_batch),
                  pl.BlockSpec((1, rows, QK_ROPE), per_batch),
                  pl.BlockSpec((1, tokens, KV_LORA), per_batch),
                  pl.BlockSpec((1, tokens, QK_ROPE), per_batch),
                  pl.BlockSpec(memory_space=pl.ANY),
                  pl.BlockSpec(memory_space=pl.ANY)],
        out_specs=pl.BlockSpec((1, rows, KV_LORA), per_batch),
        scratch_shapes=[pltpu.VMEM((SAMPLE_SLOTS, pages, PAGE_SIZE, KV_LORA), F32),
                        pltpu.VMEM((SAMPLE_SLOTS, pages, QK_ROPE, PAGE_SIZE), F32),
                        pltpu.SemaphoreType.DMA((2, SAMPLE_SLOTS))],
    )
    return pl.pallas_call(
        functools.partial(_mla_sample_kernel, pages=pages, groups=groups, tokens=tokens),
        grid_spec=grid_spec,
        out_shape=jax.ShapeDtypeStruct((DB, rows, KV_LORA), F32),
        compiler_params=_params(("arbitrary",)),
        name="mla_sample",
    )(page_table, q1, q2, cnew, knew, cache_ckv, cache_kpe_t)


def _retention_kernel(*refs, T, L, has_s0):
    C = RET_CHUNK
    padded = T < C
    if has_s0:
        q_ref, k_ref, v_ref, s0_ref, y_ref, s_ref = refs[:6]
        rest = refs[6:]
    else:
        q_ref, k_ref, v_ref, y_ref, s_ref = refs[:5]
        rest = refs[5:]
    heads = range(RET_HEADS)
    ri = lax.broadcasted_iota(jnp.int32, (C, C), 0).astype(F32)
    ci = lax.broadcasted_iota(jnp.int32, (C, C), 1).astype(F32)
    diff = ri - ci
    causal = diff >= 0.0
    log_gamma = [math.log(1.0 - 2.0 ** (-5.0 - h)) for h in heads]
    dmask = [jnp.where(causal, jnp.exp(jnp.where(causal, diff, 0.0) * lg), 0.0) for lg in log_gamma]
    q_decay = [jnp.exp((ri + 1.0) * lg) for lg in log_gamma]
    k_decay = [jnp.exp((L - 1.0 - ri) * lg) for lg in log_gamma]
    chunk_decay = [math.exp(L * lg) for lg in log_gamma]

    def col(h):
        return slice(h * RET_DK, (h + 1) * RET_DK)

    def chunk(load, store):
        q = [load(q_ref if not padded else rest[0], h).astype(BF16) for h in heads]
        k = [load(k_ref if not padded else rest[1], h) for h in heads]
        v = [load(v_ref if not padded else rest[2], h).astype(BF16) for h in heads]
        s = [s_ref[0, h] for h in heads]
        scores = [_dot_nt(q[h], k[h].astype(BF16)) for h in heads]
        cross = [_dot(q[h], s[h].astype(BF16)) for h in heads]
        kd_t = [(k[h].astype(F32) * k_decay[h]).T.astype(BF16) for h in heads]
        inner = [_dot((scores[h] * dmask[h]).astype(BF16), v[h]) for h in heads]
        upd = [_dot(kd_t[h], v[h]) for h in heads]
        for h in heads:
            y = inner[h] + cross[h] * q_decay[h]
            mu = jnp.mean(y, axis=-1, keepdims=True)
            yc = y - mu
            var = jnp.mean(yc * yc, axis=-1, keepdims=True)
            store(h, yc * lax.rsqrt(var + LN_EPS))
            s_ref[0, h] = chunk_decay[h] * s[h] + upd[h]

    for h in heads:
        s_ref[0, h] = s0_ref[0, h] if has_s0 else jnp.zeros((RET_DK, RET_DV), F32)
    if padded:
        for src, dst in zip((q_ref, k_ref, v_ref), rest):
            dst[...] = jnp.zeros(dst.shape, F32)
            dst[0:T, :] = src[0].astype(F32)

        def store(h, yn):
            y_ref[0, :, col(h)] = yn[0:T, :].astype(y_ref.dtype)
        chunk(lambda ref, h: ref[:, col(h)], store)
    else:
        def body(c, carry):
            rows = pl.ds(pl.multiple_of(c * C, C), C)

            def store(h, yn):
                y_ref[0, rows, col(h)] = yn.astype(y_ref.dtype)
            chunk(lambda ref, h: ref[0, rows, col(h)], store)
            return carry
        lax.fori_loop(0, T // C, body, 0)


def _retention(rq, rk, rv, s0, *, L, out_dtype):
    B, T, D = rq.shape
    has_s0 = s0 is not None
    padded = T < RET_CHUNK
    assert padded or T % RET_CHUNK == 0
    tok = pl.BlockSpec((1, T, D), lambda b: (b, 0, 0))
    state = pl.BlockSpec((1, RET_HEADS, RET_DK, RET_DV), lambda b: (b, 0, 0, 0))
    in_specs = [tok, tok, tok] + ([state] if has_s0 else [])
    args = (rq, rk, rv) + ((s0,) if has_s0 else ())
    return pl.pallas_call(
        functools.partial(_retention_kernel, T=T, L=L, has_s0=has_s0),
        grid=(B,),
        in_specs=in_specs,
        out_specs=[tok, state],
        out_shape=[jax.ShapeDtypeStruct((B, T, D), out_dtype),
                   jax.ShapeDtypeStruct((B, RET_HEADS, RET_DK, RET_DV), F32)],
        scratch_shapes=[pltpu.VMEM((RET_CHUNK, D), F32)] * 3 if padded else [],
        compiler_params=_params(("parallel",)),
        name="retention",
    )(*args)


def _post_kernel(*refs, apply_uv):
    if apply_uv:
        (x_ref, att_ref, y_ref, ga_ref, gr_ref, mod_ref, wuv_ref, wo_ref, wup_ref, wdn_ref,
         g1_ref, b1_ref, g2_ref, b2_ref, o_ref) = refs
    else:
        (x_ref, att_ref, y_ref, ga_ref, gr_ref, mod_ref, wo_ref, wup_ref, wdn_ref,
         g1_ref, b1_ref, g2_ref, b2_ref, o_ref) = refs
    D = D_MODEL
    x = x_ref[0]
    if apply_uv:
        o_a = jnp.concatenate([_dot(att_ref[0, head].astype(BF16), wuv_ref[head])
                               for head in range(MLA_HEADS)], axis=1)
    else:
        o_a = att_ref[0].astype(F32)
    mixv = ga_ref[0].astype(F32) * o_a + gr_ref[0].astype(F32) * y_ref[0].astype(F32)
    mix = _dot(mixv.astype(BF16), wo_ref[...])
    g1 = mod_ref[0, :, 2 * D:3 * D]
    sh2 = mod_ref[0, :, 3 * D:4 * D]
    sc2 = mod_ref[0, :, 4 * D:5 * D]
    g2 = mod_ref[0, :, 5 * D:6 * D]
    x1 = _layer_norm(ALPHA * x + (1.0 + g1) * mix, g1_ref[...], b1_ref[...])
    h2 = (x1 * (1.0 + sc2) + sh2).astype(BF16)
    m = None
    for c in range(D_FF // D):
        u = jnp.maximum(_dot(h2, wup_ref[:, c * D:(c + 1) * D]), 0.0)
        part = _dot((u * u).astype(BF16), wdn_ref[c * D:(c + 1) * D, :])
        m = part if m is None else m + part
    o_ref[0] = _layer_norm(ALPHA * x1 + (1.0 + g2) * m, g2_ref[...], b2_ref[...])


def _post(x3, att, y3, ga, gr, mod3, w, *, tm, apply_uv):
    G, Sg, D = x3.shape
    R = mod3.shape[1]
    row = lambda g, i: (g, i, 0)
    act = pl.BlockSpec((1, tm, D), row)
    if apply_uv:
        att_spec = pl.BlockSpec((1, MLA_HEADS, tm, KV_LORA), lambda g, i: (g, 0, i, 0))
    else:
        att_spec = act
    in_specs = [act, att_spec, act, act, act, pl.BlockSpec((1, R, 6 * D), lambda g, i: (g, 0, 0))]
    args = [x3, att, y3, ga, gr, mod3]
    names = (["w_uv"] if apply_uv else []) + ["w_o", "w_up", "w_down", "ln1_g", "ln1_b", "ln2_g", "ln2_b"]
    for n in names:
        in_specs.append(_resident(w[n].shape))
        args.append(w[n])
    return pl.pallas_call(
        functools.partial(_post_kernel, apply_uv=apply_uv),
        grid=(G, Sg // tm),
        in_specs=in_specs,
        out_specs=act,
        out_shape=jax.ShapeDtypeStruct((G, Sg, D), F32),
        compiler_params=_params(("parallel", "parallel")),
        name="post",
    )(*args)


def _rope_tables(pos, dim):
    half = dim // 2
    inv = ROPE_BASE ** (-jnp.arange(half, dtype=F32) / half)
    ang = pos.astype(F32)[:, None] * inv[None, :]
    cos, sin = jnp.cos(ang), jnp.sin(ang)
    reps = LANES // dim
    return (jnp.tile(jnp.concatenate([cos, cos], -1), (1, reps)),
            jnp.tile(jnp.concatenate([-sin, sin], -1), (1, reps)))


def kernel(x_prompt, x_sample, c_prompt, c_sample, cache_ckv, cache_kpe, state_ret, page_table,
           w_ada, b_ada, w_in, g_qnorm, g_kvnorm, w_uq, w_uk, w_uv, w_o,
           ln1_g, ln1_b, w_up, w_down, ln2_g, ln2_b):
    B, S, D = x_prompt.shape
    DB, T, _ = x_sample.shape
    n_pages = page_table.shape[1]
    past_len = n_pages * PAGE_SIZE
    assert DEPTH == 1 and w_in.shape[0] == 1
    l = 0
    n_head = Q_LORA + KV_LORA + QK_ROPE

    w_kr = w_in[l][:, Q_LORA + KV_LORA:n_head]
    w = {
        "w_head": jnp.concatenate([w_in[l][:, :n_head], w_kr], axis=1).astype(BF16),
        "w_r": w_in[l][:, n_head:].astype(BF16),
        "g_q": g_qnorm[l][None, :], "g_kv": g_kvnorm[l][None, :],
        "w_uq_nope": w_uq[l][:, :, :QK_NOPE].reshape(Q_LORA, MLA_HEADS * QK_NOPE).astype(BF16),
        "w_uq_pe": w_uq[l][:, :, QK_NOPE:].reshape(Q_LORA, MLA_HEADS * QK_ROPE).astype(BF16),
        "w_ukT": jnp.transpose(w_uk[l], (1, 2, 0)).astype(BF16),
        "w_uv": jnp.transpose(w_uv[l], (1, 0, 2)).astype(BF16),
        "w_o": w_o[l].astype(BF16), "w_up": w_up[l].astype(BF16), "w_down": w_down[l].astype(BF16),
        "ln1_g": ln1_g[l][None, :], "ln1_b": ln1_b[l][None, :],
        "ln2_g": ln2_g[l][None, :], "ln2_b": ln2_b[l][None, :],
    }

    n_c = B + DB
    pad = (-n_c) % 16
    c_all = jnp.concatenate([c_prompt, c_sample, jnp.zeros((pad, D), F32)], axis=0)
    ada = _ada(c_all, w_ada[l], b_ada[l][None, :])
    mod_p = ada[:B].reshape(B, 1, 6 * D)
    mod_s = jnp.repeat(ada[B:n_c], T, axis=0).reshape(1, DB * T, 6 * D)

    pos_p = jnp.arange(S)
    tabs_p = _rope_tables(pos_p, QK_ROPE) + _rope_tables(pos_p, RET_DK)
    (ckv_p, kpe_p, kc_p, kp_p, qlat_p, qpe_p, rq_p, rk_p, rv_p, ga_p, gr_p) = _inproj(
        x_prompt, mod_p, tabs_p, w, tm=256, act_dtype=BF16)
    oa_p = _mla_prompt(qlat_p, qpe_p, kc_p, kp_p, w["w_uv"], tq=256)
    y_p, ret_p = _retention(rq_p, rk_p, rv_p, None, L=min(RET_CHUNK, S), out_dtype=BF16)
    out_p = _post(x_prompt, oa_p, y_p, ga_p, gr_p, mod_p, w, tm=256, apply_uv=False)

    n_s = DB * T
    pos_s = jnp.tile(past_len + jnp.arange(T), DB)
    tabs_s = _rope_tables(pos_s, QK_ROPE) + _rope_tables(pos_s, RET_DK)
    xs = x_sample.reshape(1, n_s, D)
    (ckv_s, kpe_s, _, _, qlat_s, qpe_s, rq_s, rk_s, rv_s, ga_s, gr_s) = _inproj(
        xs, mod_s, tabs_s, w, tm=n_s, act_dtype=F32)

    def to_batch_rows(q):
        d = q.shape[-1]
        return q[0].reshape(MLA_HEADS, DB, T, d).transpose(1, 0, 2, 3).reshape(DB, MLA_HEADS * T, d)

    olat_s = _mla_sample(page_table, to_batch_rows(qlat_s), to_batch_rows(qpe_s),
                         ckv_s.reshape(DB, T, KV_LORA), kpe_s.reshape(DB, T, QK_ROPE),
                         cache_ckv, jnp.swapaxes(cache_kpe, 2, 3), pages=16)
    olat_s = olat_s.reshape(DB, MLA_HEADS, T, KV_LORA).transpose(1, 0, 2, 3).reshape(1, MLA_HEADS, n_s, KV_LORA)
    y_s, ret_s = _retention(rq_s.reshape(DB, T, D), rk_s.reshape(DB, T, D), rv_s.reshape(DB, T, D),
                            state_ret[l], L=T, out_dtype=F32)
    out_s = _post(xs, olat_s, y_s.reshape(1, n_s, D), ga_s, gr_s, mod_s, w, tm=n_s, apply_uv=True)

    return (out_p, out_s.reshape(DB, T, D),
            ckv_p[None], kpe_p[None], ret_p[None],
            ckv_s.reshape(1, DB, T, KV_LORA), kpe_s.reshape(1, DB, T, QK_ROPE), ret_s[None])
```

```python
import functools
import math

import jax
import jax.numpy as jnp
from jax import lax
from jax.experimental import pallas as pl
from jax.experimental.pallas import tpu as pltpu

D_MODEL = 1024
DEPTH = 1
PAGE_SIZE = 128
MLA_HEADS = 8
Q_LORA = 384
KV_LORA = 256
QK_NOPE = 128
QK_ROPE = 64
V_HEAD = D_MODEL // MLA_HEADS
MLA_SCALE = (QK_NOPE + QK_ROPE) ** -0.5
LOG2E = 1.4426950408889634
Q_SCALE = MLA_SCALE * LOG2E
RET_HEADS = 8
RET_DK = 128
RET_DV = D_MODEL // RET_HEADS
RET_CHUNK = 128
D_FF = 4 * D_MODEL
ROPE_BASE = 10000.0
LN_EPS = 1e-5
RMS_EPS = 1e-6
NEG = -1e30
ALPHA = (2.0 * DEPTH) ** 0.25

SCORE_LOOKAHEAD = 3
LANES = 128
V7X_VMEM_BYTES = 64 * 1024 * 1024
VMEM_LIMIT = V7X_VMEM_BYTES - 8 * 1024 * 1024

F32 = jnp.float32
BF16 = jnp.bfloat16


def _params(sem):
    return pltpu.CompilerParams(dimension_semantics=sem, vmem_limit_bytes=VMEM_LIMIT)


def _resident(shape):
    nd = len(shape)
    return pl.BlockSpec(shape, lambda *_: (0,) * nd, pipeline_mode=pl.Buffered(1))


def _dot(a, b):
    return jnp.dot(a, b, preferred_element_type=F32)


def _dot_nt(a, b):
    return lax.dot_general(a, b, (((1,), (1,)), ((), ())), preferred_element_type=F32)


def _rope(x, cos, sin_signed, half):
    lanes = x.shape[-1]
    if 2 * half == lanes:
        partner = pltpu.roll(x, half, axis=1)
    else:
        lane = lax.broadcasted_iota(jnp.int32, x.shape, 1)
        first = (lane & (2 * half - 1)) < half
        partner = jnp.where(first, pltpu.roll(x, lanes - half, axis=1), pltpu.roll(x, half, axis=1))
    return x * cos + partner * sin_signed


def _layer_norm(v, g, b):
    mu = jnp.mean(v, axis=-1, keepdims=True)
    vc = v - mu
    var = jnp.mean(vc * vc, axis=-1, keepdims=True)
    return vc * lax.rsqrt(var + LN_EPS) * g + b


def _ada_kernel(c_ref, w_ref, b_ref, o_ref):
    c = c_ref[...]
    s = (c * jax.nn.sigmoid(c)).astype(BF16)
    o_ref[...] = _dot(s, w_ref[...].astype(BF16)) + b_ref[...]


def _ada(c_all, w_ada, b_ada):
    rows = c_all.shape[0]
    tn = D_MODEL
    return pl.pallas_call(
        _ada_kernel,
        grid=(6 * D_MODEL // tn,),
        in_specs=[pl.BlockSpec((rows, D_MODEL), lambda j: (0, 0)),
                  pl.BlockSpec((D_MODEL, tn), lambda j: (0, j)),
                  pl.BlockSpec((1, tn), lambda j: (0, j))],
        out_specs=pl.BlockSpec((rows, tn), lambda j: (0, j)),
        out_shape=jax.ShapeDtypeStruct((rows, 6 * D_MODEL), F32),
        compiler_params=_params(("parallel",)),
        name="ada",
    )(c_all, w_ada, b_ada)


def _inproj_kernel(x_ref, mod_ref, cosa_ref, sina_ref, cosb_ref, sinb_ref,
                   wh_ref, wr_ref, gq_ref, gkv_ref, wqn_ref, wqp_ref, wuk_ref,
                   ckv_ref, kpe_ref, kc_ref, kp_ref, qlat_ref, qpe_ref,
                   rq_ref, rk_ref, rv_ref, ga_ref, gr_ref):
    D = D_MODEL
    x = x_ref[0]
    sh1 = mod_ref[0, :, 0:D]
    sc1 = mod_ref[0, :, D:2 * D]
    h = (x * (1.0 + sc1) + sh1).astype(BF16)

    yh = _dot(h, wh_ref[...])
    pq = yh[:, 0:Q_LORA]
    pkv = yh[:, Q_LORA:Q_LORA + KV_LORA]
    pkr = yh[:, Q_LORA + KV_LORA:Q_LORA + KV_LORA + LANES]
    qn = (pq * lax.rsqrt(jnp.mean(pq * pq, axis=-1, keepdims=True) + RMS_EPS) * gq_ref[...]).astype(BF16)
    ckv = pkv * lax.rsqrt(jnp.mean(pkv * pkv, axis=-1, keepdims=True) + RMS_EPS) * gkv_ref[...]
    ckv_ref[0] = ckv
    kc_ref[0] = ckv.astype(BF16)
    cosa = cosa_ref[...]
    sina = sina_ref[...]
    kpe = _rope(pkr, cosa, sina, QK_ROPE // 2)[:, 0:QK_ROPE]
    kpe_ref[0] = kpe
    kp_ref[0] = kpe.astype(BF16)

    qpe = _dot(qn, wqp_ref[...])
    for j in range(MLA_HEADS * QK_ROPE // LANES):
        blk = _rope(qpe[:, j * LANES:(j + 1) * LANES], cosa, sina, QK_ROPE // 2) * Q_SCALE
        for hh in range(LANES // QK_ROPE):
            head = j * (LANES // QK_ROPE) + hh
            qpe_ref[0, head] = blk[:, hh * QK_ROPE:(hh + 1) * QK_ROPE].astype(BF16)
    qnope = _dot(qn, wqn_ref[...])
    for head in range(MLA_HEADS):
        qh = qnope[:, head * QK_NOPE:(head + 1) * QK_NOPE].astype(BF16)
        qlat_ref[0, head] = (_dot(qh, wuk_ref[head]) * Q_SCALE).astype(BF16)

    cosb = cosb_ref[...]
    sinb = sinb_ref[...]
    rq = _dot(h, wr_ref[:, 0:D])
    rk = _dot(h, wr_ref[:, D:2 * D])
    for head in range(RET_HEADS):
        sl = slice(head * RET_DK, (head + 1) * RET_DK)
        rq_ref[0, :, sl] = _rope(rq[:, sl], cosb, sinb, RET_DK // 2).astype(rq_ref.dtype)
        rk_ref[0, :, sl] = (_rope(rk[:, sl], cosb, sinb, RET_DK // 2) * (RET_DK ** -0.5)).astype(rk_ref.dtype)
    rv_ref[0] = _dot(h, wr_ref[:, 2 * D:3 * D]).astype(rv_ref.dtype)
    rg = _dot(h, wr_ref[:, 3 * D:4 * D])
    ga_ref[0] = jax.nn.sigmoid(_dot(h, wr_ref[:, 4 * D:5 * D])).astype(BF16)
    gr = jax.nn.sigmoid(_dot(h, wr_ref[:, 5 * D:6 * D]))
    gr_ref[0] = (gr * (rg * jax.nn.sigmoid(rg))).astype(BF16)


def _inproj(x3, mod3, tabs, w, *, tm, act_dtype):
    G, Sg, D = x3.shape
    R = mod3.shape[1]
    assert Sg % tm == 0 and R in (1, tm) and (R == 1 or Sg == tm)
    row = lambda g, i: (g, i, 0)
    head_row = lambda g, i: (g, 0, i, 0)
    tab = pl.BlockSpec((tm, LANES), lambda g, i: (i, 0))
    in_specs = [
        pl.BlockSpec((1, tm, D), row),
        pl.BlockSpec((1, R, 2 * D), lambda g, i: (g, 0, 0)),
        tab, tab, tab, tab,
        _resident(w["w_head"].shape), _resident(w["w_r"].shape),
        _resident(w["g_q"].shape), _resident(w["g_kv"].shape),
        _resident(w["w_uq_nope"].shape), _resident(w["w_uq_pe"].shape), _resident(w["w_ukT"].shape),
    ]
    out_shape = [
        jax.ShapeDtypeStruct((G, Sg, KV_LORA), F32), jax.ShapeDtypeStruct((G, Sg, QK_ROPE), F32),
        jax.ShapeDtypeStruct((G, Sg, KV_LORA), BF16), jax.ShapeDtypeStruct((G, Sg, QK_ROPE), BF16),
        jax.ShapeDtypeStruct((G, MLA_HEADS, Sg, KV_LORA), BF16),
        jax.ShapeDtypeStruct((G, MLA_HEADS, Sg, QK_ROPE), BF16),
        jax.ShapeDtypeStruct((G, Sg, D), act_dtype), jax.ShapeDtypeStruct((G, Sg, D), act_dtype),
        jax.ShapeDtypeStruct((G, Sg, D), act_dtype),
        jax.ShapeDtypeStruct((G, Sg, D), BF16), jax.ShapeDtypeStruct((G, Sg, D), BF16),
    ]
    out_specs = [
        pl.BlockSpec((1, tm, KV_LORA), row), pl.BlockSpec((1, tm, QK_ROPE), row),
        pl.BlockSpec((1, tm, KV_LORA), row), pl.BlockSpec((1, tm, QK_ROPE), row),
        pl.BlockSpec((1, MLA_HEADS, tm, KV_LORA), head_row),
        pl.BlockSpec((1, MLA_HEADS, tm, QK_ROPE), head_row),
        pl.BlockSpec((1, tm, D), row), pl.BlockSpec((1, tm, D), row), pl.BlockSpec((1, tm, D), row),
        pl.BlockSpec((1, tm, D), row), pl.BlockSpec((1, tm, D), row),
    ]
    return pl.pallas_call(
        _inproj_kernel,
        grid=(G, Sg // tm),
        in_specs=in_specs, out_specs=out_specs, out_shape=out_shape,
        compiler_params=_params(("parallel", "parallel")),
        name="inproj",
    )(x3, mod3, *tabs, w["w_head"], w["w_r"], w["g_q"], w["g_kv"],
      w["w_uq_nope"], w["w_uq_pe"], w["w_ukT"])


def _mla_prompt_kernel(qlat_ref, qpe_ref, kc_ref, kp_ref, wuv_ref, o_ref,
                       m_ref, l_ref, acc_ref, s_ref, *, tq):
    qi = pl.program_id(1)

    def kv(j):
        start = pl.multiple_of(j * tq, tq)
        return kc_ref[0, pl.ds(start, tq), :], kp_ref[0, pl.ds(start, tq), :]

    def scores(head, kc, kp):
        return _dot_nt(qlat_ref[0, head], kc) + _dot_nt(qpe_ref[0, head], kp)

    kc0, kp0 = kv(0)
    for hd in range(SCORE_LOOKAHEAD):
        s_ref[hd] = scores(hd, kc0, kp0)

    def tile(j, first, last):
        kc, kp = kv(j)
        col = lax.broadcasted_iota(jnp.int32, (tq, tq), 1)
        row = lax.broadcasted_iota(jnp.int32, (tq, tq), 0)
        if first:
            keep = col <= row + qi * tq
        elif last:
            keep = col <= row
        if not last:
            kc_next, kp_next = kv(j + 1)
        pending = [s_ref[hd] for hd in range(SCORE_LOOKAHEAD)]
        for head in range(MLA_HEADS):
            s = pending.pop(0)
            ahead = head + SCORE_LOOKAHEAD
            if ahead < MLA_HEADS:
                pending.append(scores(ahead, kc, kp))
            elif not last:
                s_ref[ahead - MLA_HEADS] = scores(ahead - MLA_HEADS, kc_next, kp_next)
            if first or last:
                s = jnp.where(keep, s, NEG)
            m_cur = jnp.max(s, axis=-1, keepdims=True)
            m_new = jnp.broadcast_to(m_cur, (tq, LANES)) if first else jnp.maximum(m_ref[head], m_cur)
            p = jnp.exp2(s - jnp.concatenate([m_new] * (tq // LANES), axis=1))
            psum = p[:, 0:LANES]
            for c in range(1, tq // LANES):
                psum = psum + p[:, c * LANES:(c + 1) * LANES]
            pv = _dot(p.astype(BF16), kc)
            if first:
                l_ref[head] = psum
                acc_ref[head] = pv
            else:
                a = jnp.exp2(m_ref[head] - m_new)
                l_ref[head] = a * l_ref[head] + psum
                acc_ref[head] = jnp.concatenate([a] * (KV_LORA // LANES), axis=1) * acc_ref[head] + pv
            m_ref[head] = m_new

    def body(j, carry):
        tile(j, False, False)
        return carry

    tile(0, True, False)
    lax.fori_loop(1, qi, body, 0)

    @pl.when(qi > 0)
    def _():
        tile(qi, False, True)

    heads = range(MLA_HEADS)
    inv = [1.0 / jnp.sum(l_ref[head], axis=-1, keepdims=True) for head in heads]
    lat = [(acc_ref[head] * inv[head]).astype(BF16) for head in heads]
    out = [_dot(lat[head], wuv_ref[head]) for head in heads]
    for head in heads:
        o_ref[0, :, head * V_HEAD:(head + 1) * V_HEAD] = out[head].astype(BF16)


def _mla_prompt(qlat, qpe, kc, kp, w_uv, *, tq):
    B, Hh, S, _ = qlat.shape
    assert tq % LANES == 0 and S >= 2 * tq
    return pl.pallas_call(
        functools.partial(_mla_prompt_kernel, tq=tq),
        grid=(B, S // tq),
        in_specs=[pl.BlockSpec((1, Hh, tq, KV_LORA), lambda b, i: (b, 0, i, 0)),
                  pl.BlockSpec((1, Hh, tq, QK_ROPE), lambda b, i: (b, 0, i, 0)),
                  pl.BlockSpec((1, S, KV_LORA), lambda b, i: (b, 0, 0)),
                  pl.BlockSpec((1, S, QK_ROPE), lambda b, i: (b, 0, 0)),
                  _resident(w_uv.shape)],
        out_specs=pl.BlockSpec((1, tq, D_MODEL), lambda b, i: (b, i, 0)),
        out_shape=jax.ShapeDtypeStruct((B, S, D_MODEL), BF16),
        scratch_shapes=[pltpu.VMEM((Hh, tq, LANES), F32), pltpu.VMEM((Hh, tq, LANES), F32),
                        pltpu.VMEM((Hh, tq, KV_LORA), F32), pltpu.VMEM((SCORE_LOOKAHEAD, tq, tq), F32)],
        compiler_params=_params(("parallel", "parallel")),
        name="mla_prompt",
    )(qlat, qpe, kc, kp, w_uv)


def _mla_sample_kernel(pt_ref, q1_ref, q2_ref, cn_ref, kn_ref, ck_hbm, kp_hbm, o_ref,
                       ck_buf, kp_buf, sems, *, pages, groups, tokens):
    b = pl.program_id(0)
    n_b = pl.num_programs(0)
    n_pages = pages * groups

    def page_copies(bb, i):
        slot = lax.rem(bb, 2)
        page = pt_ref[bb, i]
        return (pltpu.make_async_copy(ck_hbm.at[0, page], ck_buf.at[slot, i], sems.at[0, slot]),
                pltpu.make_async_copy(kp_hbm.at[0, page], kp_buf.at[slot, i], sems.at[1, slot]))

    @pl.when(b == 0)
    def _():
        for i in range(n_pages):
            for cp in page_copies(b, i):
                cp.start()

    for i in range(n_pages):
        for cp in page_copies(b, i):
            cp.wait()

    def start_next_row(i):
        @pl.when(b + 1 < n_b)
        def _():
            for cp in page_copies(b + 1, i):
                cp.start()

    q1 = q1_ref[0]
    q2 = q2_ref[0]
    rows = q1.shape[0]
    slot = lax.rem(b, 2)

    def scores(g):
        cols = []
        for i in range(g * pages, (g + 1) * pages):
            cols.append(_dot_nt(q1, ck_buf[slot, i].astype(BF16)) + _dot(q2, kp_buf[slot, i].astype(BF16)))
            start_next_row(i)
        return cols

    def update(g, cols, m, l, acc):
        mx = cols[0]
        for c in cols[1:]:
            mx = jnp.maximum(mx, c)
        m_new = jnp.maximum(m, jnp.max(mx, axis=-1, keepdims=True))
        a = jnp.exp2(m - m_new)
        ps = [jnp.exp2(c - m_new) for c in cols]
        psum = ps[0]
        for p in ps[1:]:
            psum = psum + p
        pv = None
        for j in range(pages):
            part = _dot(ps[j].astype(BF16), ck_buf[slot, g * pages + j].astype(BF16))
            pv = part if pv is None else pv + part
        acc = jnp.concatenate([a] * (KV_LORA // LANES), axis=1) * acc + pv
        return m_new, a * l + psum, acc

    m = jnp.full((rows, LANES), -jnp.inf, F32)
    l = jnp.zeros((rows, LANES), F32)
    acc = jnp.zeros((rows, KV_LORA), F32)
    pending = scores(0)
    for g in range(groups):
        nxt = scores(g + 1) if g + 1 < groups else None
        m, l, acc = update(g, pending, m, l, acc)
        pending = nxt

    q1f = q1.astype(F32)
    q2f = q2.astype(F32)
    cn = cn_ref[0]
    kn = kn_ref[0]
    tok = lax.broadcasted_iota(jnp.int32, (rows, LANES), 0) & (tokens - 1)
    cols = []
    for t in range(tokens):
        sc = (jnp.sum(q1f * cn[t:t + 1, :], axis=-1, keepdims=True)
              + jnp.sum(q2f * kn[t:t + 1, :], axis=-1, keepdims=True))
        cols.append(jnp.where(tok >= t, sc, NEG))
    m_fin = m
    for sc in cols:
        m_fin = jnp.maximum(m_fin, sc)
    a2 = jnp.exp2(m - m_fin)
    l_fin = jnp.sum(a2 * l, axis=-1, keepdims=True)
    acc = jnp.concatenate([a2] * (KV_LORA // LANES), axis=1) * acc
    for t, sc in enumerate(cols):
        pt = jnp.exp2(sc - m_fin)
        l_fin = l_fin + pt[:, 0:1]
        acc = acc + jnp.concatenate([pt] * (KV_LORA // LANES), axis=1) * cn[t:t + 1, :]
    o_ref[0] = acc * (1.0 / l_fin)


def _mla_sample(page_table, q1, q2, cnew, knew, cache_ckv, cache_kpe_t, *, pages):
    DB, n_pages = page_table.shape
    rows = q1.shape[1]
    tokens = cnew.shape[1]
    assert n_pages % pages == 0 and tokens & (tokens - 1) == 0
    groups = n_pages // pages

    per_batch = lambda b, pt: (b, 0, 0)
    grid_spec = pltpu.PrefetchScalarGridSpec(
        num_scalar_prefetch=1,
        grid=(DB,),
        in_specs=[pl.BlockSpec((1, rows, KV_LORA), per_batch),
                  pl.BlockSpec((1, rows, QK_ROPE), per_batch),
                  pl.BlockSpec((1, tokens, KV_LORA), per_batch),
                  pl.BlockSpec((1, tokens, QK_ROPE), per_batch),
                  pl.BlockSpec(memory_space=pl.ANY),
                  pl.BlockSpec(memory_space=pl.ANY)],
        out_specs=pl.BlockSpec((1, rows, KV_LORA), per_batch),
        scratch_shapes=[pltpu.VMEM((2, n_pages, PAGE_SIZE, KV_LORA), F32),
                        pltpu.VMEM((2, n_pages, QK_ROPE, PAGE_SIZE), F32),
                        pltpu.SemaphoreType.DMA((2, 2))],
    )
    return pl.pallas_call(
        functools.partial(_mla_sample_kernel, pages=pages, groups=groups, tokens=tokens),
        grid_spec=grid_spec,
        out_shape=jax.ShapeDtypeStruct((DB, rows, KV_LORA), F32),
        compiler_params=_params(("arbitrary",)),
        name="mla_sample",
    )(page_table, q1, q2, cnew, knew, cache_ckv, cache_kpe_t)


def _retention_kernel(*refs, T, L, has_s0):
    C = RET_CHUNK
    padded = T < C
    if has_s0:
        q_ref, k_ref, v_ref, s0_ref, y_ref, s_ref = refs[:6]
        rest = refs[6:]
    else:
        q_ref, k_ref, v_ref, y_ref, s_ref = refs[:5]
        rest = refs[5:]
    heads = range(RET_HEADS)
    ri = lax.broadcasted_iota(jnp.int32, (C, C), 0).astype(F32)
    ci = lax.broadcasted_iota(jnp.int32, (C, C), 1).astype(F32)
    diff = ri - ci
    causal = diff >= 0.0
    log_gamma = [math.log(1.0 - 2.0 ** (-5.0 - h)) for h in heads]
    dmask = [jnp.where(causal, jnp.exp(jnp.where(causal, diff, 0.0) * lg), 0.0) for lg in log_gamma]
    q_decay = [jnp.exp((ri + 1.0) * lg) for lg in log_gamma]
    k_decay = [jnp.exp((L - 1.0 - ri) * lg) for lg in log_gamma]
    chunk_decay = [math.exp(L * lg) for lg in log_gamma]

    def col(h):
        return slice(h * RET_DK, (h + 1) * RET_DK)

    def chunk(load, store):
        q = [load(q_ref if not padded else rest[0], h).astype(BF16) for h in heads]
        k = [load(k_ref if not padded else rest[1], h) for h in heads]
        v = [load(v_ref if not padded else rest[2], h).astype(BF16) for h in heads]
        s = [s_ref[0, h] for h in heads]
        scores = [_dot_nt(q[h], k[h].astype(BF16)) for h in heads]
        cross = [_dot(q[h], s[h].astype(BF16)) for h in heads]
        kd_t = [(k[h].astype(F32) * k_decay[h]).T.astype(BF16) for h in heads]
        inner = [_dot((scores[h] * dmask[h]).astype(BF16), v[h]) for h in heads]
        upd = [_dot(kd_t[h], v[h]) for h in heads]
        for h in heads:
            y = inner[h] + cross[h] * q_decay[h]
            mu = jnp.mean(y, axis=-1, keepdims=True)
            yc = y - mu
            var = jnp.mean(yc * yc, axis=-1, keepdims=True)
            store(h, yc * lax.rsqrt(var + LN_EPS))
            s_ref[0, h] = chunk_decay[h] * s[h] + upd[h]

    for h in heads:
        s_ref[0, h] = s0_ref[0, h] if has_s0 else jnp.zeros((RET_DK, RET_DV), F32)
    if padded:
        for src, dst in zip((q_ref, k_ref, v_ref), rest):
            dst[...] = jnp.zeros(dst.shape, F32)
            dst[0:T, :] = src[0].astype(F32)

        def store(h, yn):
            y_ref[0, :, col(h)] = yn[0:T, :].astype(y_ref.dtype)
        chunk(lambda ref, h: ref[:, col(h)], store)
    else:
        def body(c, carry):
            rows = pl.ds(pl.multiple_of(c * C, C), C)

            def store(h, yn):
                y_ref[0, rows, col(h)] = yn.astype(y_ref.dtype)
            chunk(lambda ref, h: ref[0, rows, col(h)], store)
            return carry
        lax.fori_loop(0, T // C, body, 0)


def _retention(rq, rk, rv, s0, *, L, out_dtype):
    B, T, D = rq.shape
    has_s0 = s0 is not None
    padded = T < RET_CHUNK
    assert padded or T % RET_CHUNK == 0
    tok = pl.BlockSpec((1, T, D), lambda b: (b, 0, 0))
    state = pl.BlockSpec((1, RET_HEADS, RET_DK, RET_DV), lambda b: (b, 0, 0, 0))
    in_specs = [tok, tok, tok] + ([state] if has_s0 else [])
    args = (rq, rk, rv) + ((s0,) if has_s0 else ())
    return pl.pallas_call(
        functools.partial(_retention_kernel, T=T, L=L, has_s0=has_s0),
        grid=(B,),
        in_specs=in_specs,
        out_specs=[tok, state],
        out_shape=[jax.ShapeDtypeStruct((B, T, D), out_dtype),
                   jax.ShapeDtypeStruct((B, RET_HEADS, RET_DK, RET_DV), F32)],
        scratch_shapes=[pltpu.VMEM((RET_CHUNK, D), F32)] * 3 if padded else [],
        compiler_params=_params(("parallel",)),
        name="retention",
    )(*args)


def _post_kernel(*refs, apply_uv):
    if apply_uv:
        (x_ref, att_ref, y_ref, ga_ref, gr_ref, mod_ref, wuv_ref, wo_ref, wup_ref, wdn_ref,
         g1_ref, b1_ref, g2_ref, b2_ref, o_ref) = refs
    else:
        (x_ref, att_ref, y_ref, ga_ref, gr_ref, mod_ref, wo_ref, wup_ref, wdn_ref,
         g1_ref, b1_ref, g2_ref, b2_ref, o_ref) = refs
    D = D_MODEL
    x = x_ref[0]
    if apply_uv:
        o_a = jnp.concatenate([_dot(att_ref[0, head].astype(BF16), wuv_ref[head])
                               for head in range(MLA_HEADS)], axis=1)
    else:
        o_a = att_ref[0].astype(F32)
    mixv = ga_ref[0].astype(F32) * o_a + gr_ref[0].astype(F32) * y_ref[0].astype(F32)
    mix = _dot(mixv.astype(BF16), wo_ref[...])
    g1 = mod_ref[0, :, 2 * D:3 * D]
    sh2 = mod_ref[0, :, 3 * D:4 * D]
    sc2 = mod_ref[0, :, 4 * D:5 * D]
    g2 = mod_ref[0, :, 5 * D:6 * D]
    x1 = _layer_norm(ALPHA * x + (1.0 + g1) * mix, g1_ref[...], b1_ref[...])
    h2 = (x1 * (1.0 + sc2) + sh2).astype(BF16)
    m = None
    for c in range(D_FF // D):
        u = jnp.maximum(_dot(h2, wup_ref[:, c * D:(c + 1) * D]), 0.0)
        part = _dot((u * u).astype(BF16), wdn_ref[c * D:(c + 1) * D, :])
        m = part if m is None else m + part
    o_ref[0] = _layer_norm(ALPHA * x1 + (1.0 + g2) * m, g2_ref[...], b2_ref[...])


def _post(x3, att, y3, ga, gr, mod3, w, *, tm, apply_uv):
    G, Sg, D = x3.shape
    R = mod3.shape[1]
    row = lambda g, i: (g, i, 0)
    act = pl.BlockSpec((1, tm, D), row)
    if apply_uv:
        att_spec = pl.BlockSpec((1, MLA_HEADS, tm, KV_LORA), lambda g, i: (g, 0, i, 0))
    else:
        att_spec = act
    in_specs = [act, att_spec, act, act, act, pl.BlockSpec((1, R, 6 * D), lambda g, i: (g, 0, 0))]
    args = [x3, att, y3, ga, gr, mod3]
    names = (["w_uv"] if apply_uv else []) + ["w_o", "w_up", "w_down", "ln1_g", "ln1_b", "ln2_g", "ln2_b"]
    for n in names:
        in_specs.append(_resident(w[n].shape))
        args.append(w[n])
    return pl.pallas_call(
        functools.partial(_post_kernel, apply_uv=apply_uv),
        grid=(G, Sg // tm),
        in_specs=in_specs,
        out_specs=act,
        out_shape=jax.ShapeDtypeStruct((G, Sg, D), F32),
        compiler_params=_params(("parallel", "parallel")),
        name="post",
    )(*args)


def _rope_tables(pos, dim):
    half = dim // 2
    inv = ROPE_BASE ** (-jnp.arange(half, dtype=F32) / half)
    ang = pos.astype(F32)[:, None] * inv[None, :]
    cos, sin = jnp.cos(ang), jnp.sin(ang)
    reps = LANES // dim
    return (jnp.tile(jnp.concatenate([cos, cos], -1), (1, reps)),
            jnp.tile(jnp.concatenate([-sin, sin], -1), (1, reps)))


def kernel(x_prompt, x_sample, c_prompt, c_sample, cache_ckv, cache_kpe, state_ret, page_table,
           w_ada, b_ada, w_in, g_qnorm, g_kvnorm, w_uq, w_uk, w_uv, w_o,
           ln1_g, ln1_b, w_up, w_down, ln2_g, ln2_b):
    B, S, D = x_prompt.shape
    DB, T, _ = x_sample.shape
    n_pages = page_table.shape[1]
    past_len = n_pages * PAGE_SIZE
    assert DEPTH == 1 and w_in.shape[0] == 1
    l = 0
    n_head = Q_LORA + KV_LORA + QK_ROPE

    w_kr = w_in[l][:, Q_LORA + KV_LORA:n_head]
    w = {
        "w_head": jnp.concatenate([w_in[l][:, :n_head], w_kr], axis=1).astype(BF16),
        "w_r": w_in[l][:, n_head:].astype(BF16),
        "g_q": g_qnorm[l][None, :], "g_kv": g_kvnorm[l][None, :],
        "w_uq_nope": w_uq[l][:, :, :QK_NOPE].reshape(Q_LORA, MLA_HEADS * QK_NOPE).astype(BF16),
        "w_uq_pe": w_uq[l][:, :, QK_NOPE:].reshape(Q_LORA, MLA_HEADS * QK_ROPE).astype(BF16),
        "w_ukT": jnp.transpose(w_uk[l], (1, 2, 0)).astype(BF16),
        "w_uv": jnp.transpose(w_uv[l], (1, 0, 2)).astype(BF16),
        "w_o": w_o[l].astype(BF16), "w_up": w_up[l].astype(BF16), "w_down": w_down[l].astype(BF16),
        "ln1_g": ln1_g[l][None, :], "ln1_b": ln1_b[l][None, :],
        "ln2_g": ln2_g[l][None, :], "ln2_b": ln2_b[l][None, :],
    }

    n_c = B + DB
    pad = (-n_c) % 16
    c_all = jnp.concatenate([c_prompt, c_sample, jnp.zeros((pad, D), F32)], axis=0)
    ada = _ada(c_all, w_ada[l], b_ada[l][None, :])
    mod_p = ada[:B].reshape(B, 1, 6 * D)
    mod_s = jnp.repeat(ada[B:n_c], T, axis=0).reshape(1, DB * T, 6 * D)

    pos_p = jnp.arange(S)
    tabs_p = _rope_tables(pos_p, QK_ROPE) + _rope_tables(pos_p, RET_DK)
    (ckv_p, kpe_p, kc_p, kp_p, qlat_p, qpe_p, rq_p, rk_p, rv_p, ga_p, gr_p) = _inproj(
        x_prompt, mod_p, tabs_p, w, tm=256, act_dtype=BF16)
    oa_p = _mla_prompt(qlat_p, qpe_p, kc_p, kp_p, w["w_uv"], tq=256)
    y_p, ret_p = _retention(rq_p, rk_p, rv_p, None, L=min(RET_CHUNK, S), out_dtype=BF16)
    out_p = _post(x_prompt, oa_p, y_p, ga_p, gr_p, mod_p, w, tm=512, apply_uv=False)

    n_s = DB * T
    pos_s = jnp.tile(past_len + jnp.arange(T), DB)
    tabs_s = _rope_tables(pos_s, QK_ROPE) + _rope_tables(pos_s, RET_DK)
    xs = x_sample.reshape(1, n_s, D)
    (ckv_s, kpe_s, _, _, qlat_s, qpe_s, rq_s, rk_s, rv_s, ga_s, gr_s) = _inproj(
        xs, mod_s, tabs_s, w, tm=n_s, act_dtype=F32)

    def to_batch_rows(q):
        d = q.shape[-1]
        return q[0].reshape(MLA_HEADS, DB, T, d).transpose(1, 0, 2, 3).reshape(DB, MLA_HEADS * T, d)

    olat_s = _mla_sample(page_table, to_batch_rows(qlat_s), to_batch_rows(qpe_s),
                         ckv_s.reshape(DB, T, KV_LORA), kpe_s.reshape(DB, T, QK_ROPE),
                         cache_ckv, jnp.swapaxes(cache_kpe, 2, 3), pages=16)
    olat_s = olat_s.reshape(DB, MLA_HEADS, T, KV_LORA).transpose(1, 0, 2, 3).reshape(1, MLA_HEADS, n_s, KV_LORA)
    y_s, ret_s = _retention(rq_s.reshape(DB, T, D), rk_s.reshape(DB, T, D), rv_s.reshape(DB, T, D),
                            state_ret[l], L=T, out_dtype=F32)
    out_s = _post(xs, olat_s, y_s.reshape(1, n_s, D), ga_s, gr_s, mod_s, w, tm=n_s, apply_uv=True)

    return (out_p, out_s.reshape(DB, T, D),
            ckv_p[None], kpe_p[None], ret_p[None],
            ckv_s.reshape(1, DB, T, KV_LORA), kpe_s.reshape(1, DB, T, QK_ROPE), ret_s[None])
```

```python
import functools
import math

import jax
import jax.numpy as jnp
from jax import lax
from jax.experimental import pallas as pl
from jax.experimental.pallas import tpu as pltpu

D_MODEL = 1024
DEPTH = 1
PAGE_SIZE = 128
MLA_HEADS = 8
Q_LORA = 384
KV_LORA = 256
QK_NOPE = 128
QK_ROPE = 64
V_HEAD = D_MODEL // MLA_HEADS
MLA_SCALE = (QK_NOPE + QK_ROPE) ** -0.5
LOG2E = 1.4426950408889634
Q_SCALE = MLA_SCALE * LOG2E
RET_HEADS = 8
RET_DK = 128
RET_DV = D_MODEL // RET_HEADS
RET_CHUNK = 128
D_FF = 4 * D_MODEL
ROPE_BASE = 10000.0
LN_EPS = 1e-5
RMS_EPS = 1e-6
NEG = -1e30
ALPHA = (2.0 * DEPTH) ** 0.25

SCORE_LOOKAHEAD = 3
LANES = 128
V7X_VMEM_BYTES = 64 * 1024 * 1024
VMEM_LIMIT = V7X_VMEM_BYTES - 8 * 1024 * 1024

F32 = jnp.float32
BF16 = jnp.bfloat16


def _params(sem):
    return pltpu.CompilerParams(dimension_semantics=sem, vmem_limit_bytes=VMEM_LIMIT)


def _resident(shape):
    nd = len(shape)
    return pl.BlockSpec(shape, lambda *_: (0,) * nd, pipeline_mode=pl.Buffered(1))


def _dot(a, b):
    return jnp.dot(a, b, preferred_element_type=F32)


def _dot_nt(a, b):
    return lax.dot_general(a, b, (((1,), (1,)), ((), ())), preferred_element_type=F32)


def _rope(x, cos, sin_signed, half):
    lanes = x.shape[-1]
    if 2 * half == lanes:
        partner = pltpu.roll(x, half, axis=1)
    else:
        lane = lax.broadcasted_iota(jnp.int32, x.shape, 1)
        first = (lane & (2 * half - 1)) < half
        partner = jnp.where(first, pltpu.roll(x, lanes - half, axis=1), pltpu.roll(x, half, axis=1))
    return x * cos + partner * sin_signed


def _layer_norm(v, g, b):
    mu = jnp.mean(v, axis=-1, keepdims=True)
    vc = v - mu
    var = jnp.mean(vc * vc, axis=-1, keepdims=True)
    return vc * lax.rsqrt(var + LN_EPS) * g + b


def _ada_kernel(c_ref, w_ref, b_ref, o_ref):
    c = c_ref[...]
    s = (c * jax.nn.sigmoid(c)).astype(BF16)
    o_ref[...] = _dot(s, w_ref[...].astype(BF16)) + b_ref[...]


def _ada(c_all, w_ada, b_ada):
    rows = c_all.shape[0]
    tn = D_MODEL
    return pl.pallas_call(
        _ada_kernel,
        grid=(6 * D_MODEL // tn,),
        in_specs=[pl.BlockSpec((rows, D_MODEL), lambda j: (0, 0)),
                  pl.BlockSpec((D_MODEL, tn), lambda j: (0, j)),
                  pl.BlockSpec((1, tn), lambda j: (0, j))],
        out_specs=pl.BlockSpec((rows, tn), lambda j: (0, j)),
        out_shape=jax.ShapeDtypeStruct((rows, 6 * D_MODEL), F32),
        compiler_params=_params(("parallel",)),
        name="ada",
    )(c_all, w_ada, b_ada)


def _inproj_kernel(x_ref, mod_ref, cosa_ref, sina_ref, cosb_ref, sinb_ref,
                   wh_ref, wr_ref, gq_ref, gkv_ref, wqn_ref, wqp_ref, wuk_ref,
                   ckv_ref, kpe_ref, kc_ref, *refs, absorb):
    if absorb:
        kp_ref, qlat_ref, qpe_ref, rq_ref, rk_ref, rv_ref, ga_ref, gr_ref = refs
    else:
        qf_ref, kf_ref, rq_ref, rk_ref, rv_ref, ga_ref, gr_ref = refs
    D = D_MODEL
    x = x_ref[0]
    sh1 = mod_ref[0, :, 0:D]
    sc1 = mod_ref[0, :, D:2 * D]
    h = (x * (1.0 + sc1) + sh1).astype(BF16)

    yh = _dot(h, wh_ref[...])
    pq = yh[:, 0:Q_LORA]
    pkv = yh[:, Q_LORA:Q_LORA + KV_LORA]
    pkr = yh[:, Q_LORA + KV_LORA:Q_LORA + KV_LORA + LANES]
    qn = (pq * lax.rsqrt(jnp.mean(pq * pq, axis=-1, keepdims=True) + RMS_EPS) * gq_ref[...]).astype(BF16)
    ckv = pkv * lax.rsqrt(jnp.mean(pkv * pkv, axis=-1, keepdims=True) + RMS_EPS) * gkv_ref[...]
    ckv_ref[0] = ckv
    kc_ref[0] = ckv.astype(BF16)
    cosa = cosa_ref[...]
    sina = sina_ref[...]
    kpe2 = _rope(pkr, cosa, sina, QK_ROPE // 2)
    kpe_ref[0] = kpe2[:, 0:QK_ROPE]

    qpe = _dot(qn, wqp_ref[...])
    qnope = _dot(qn, wqn_ref[...])
    pe_blocks = [_rope(qpe[:, j * LANES:(j + 1) * LANES], cosa, sina, QK_ROPE // 2) * Q_SCALE
                 for j in range(MLA_HEADS * QK_ROPE // LANES)]
    if absorb:
        kp_ref[0] = kpe2[:, 0:QK_ROPE].astype(BF16)
        for head in range(MLA_HEADS):
            off = (head % 2) * QK_ROPE
            qpe_ref[0, head] = pe_blocks[head // 2][:, off:off + QK_ROPE].astype(BF16)
            qh = qnope[:, head * QK_NOPE:(head + 1) * QK_NOPE].astype(BF16)
            qlat_ref[0, head] = (_dot(qh, wuk_ref[head]) * Q_SCALE).astype(BF16)
    else:
        lane = lax.broadcasted_iota(jnp.int32, kpe2.shape, 1)
        kpe_pad = jnp.where(lane < QK_ROPE, kpe2, 0.0).astype(BF16)
        knope = _dot(ckv.astype(BF16), wuk_ref[...])
        for head in range(MLA_HEADS):
            sl = slice(head * QK_NOPE, (head + 1) * QK_NOPE)
            kf_ref[0, head, :, 0:QK_NOPE] = knope[:, sl].astype(BF16)
            kf_ref[0, head, :, QK_NOPE:2 * QK_NOPE] = kpe_pad
            qf_ref[0, head, :, 0:QK_NOPE] = (qnope[:, sl] * Q_SCALE).astype(BF16)
            blk = pe_blocks[head // 2]
            if head % 2:
                blk = pltpu.roll(blk, QK_ROPE, axis=1)
            qf_ref[0, head, :, QK_NOPE:2 * QK_NOPE] = blk.astype(BF16)

    cosb = cosb_ref[...]
    sinb = sinb_ref[...]
    rq = _dot(h, wr_ref[:, 0:D])
    rk = _dot(h, wr_ref[:, D:2 * D])
    for head in range(RET_HEADS):
        sl = slice(head * RET_DK, (head + 1) * RET_DK)
        rq_ref[0, :, sl] = _rope(rq[:, sl], cosb, sinb, RET_DK // 2).astype(rq_ref.dtype)
        rk_ref[0, :, sl] = (_rope(rk[:, sl], cosb, sinb, RET_DK // 2) * (RET_DK ** -0.5)).astype(rk_ref.dtype)
    rv_ref[0] = _dot(h, wr_ref[:, 2 * D:3 * D]).astype(rv_ref.dtype)
    rg = _dot(h, wr_ref[:, 3 * D:4 * D])
    ga_ref[0] = jax.nn.sigmoid(_dot(h, wr_ref[:, 4 * D:5 * D])).astype(BF16)
    gr = jax.nn.sigmoid(_dot(h, wr_ref[:, 5 * D:6 * D]))
    gr_ref[0] = (gr * (rg * jax.nn.sigmoid(rg))).astype(BF16)


def _inproj(x3, mod3, tabs, w, *, tm, act_dtype, absorb):
    G, Sg, D = x3.shape
    R = mod3.shape[1]
    assert Sg % tm == 0 and R in (1, tm) and (R == 1 or Sg == tm)
    row = lambda g, i: (g, i, 0)
    head_row = lambda g, i: (g, 0, i, 0)
    tab = pl.BlockSpec((tm, LANES), lambda g, i: (i, 0))
    w_uk = w["w_ukT"] if absorb else w["w_uk_flat"]
    in_specs = [
        pl.BlockSpec((1, tm, D), row),
        pl.BlockSpec((1, R, 2 * D), lambda g, i: (g, 0, 0)),
        tab, tab, tab, tab,
        _resident(w["w_head"].shape), _resident(w["w_r"].shape),
        _resident(w["g_q"].shape), _resident(w["g_kv"].shape),
        _resident(w["w_uq_nope"].shape), _resident(w["w_uq_pe"].shape), _resident(w_uk.shape),
    ]

    def tok(width, dtype):
        return jax.ShapeDtypeStruct((G, Sg, width), dtype), pl.BlockSpec((1, tm, width), row)

    def per_head(width):
        return (jax.ShapeDtypeStruct((G, MLA_HEADS, Sg, width), BF16),
                pl.BlockSpec((1, MLA_HEADS, tm, width), head_row))

    outs = [tok(KV_LORA, F32), tok(QK_ROPE, F32), tok(KV_LORA, BF16)]
    if absorb:
        outs += [tok(QK_ROPE, BF16), per_head(KV_LORA), per_head(QK_ROPE)]
    else:
        outs += [per_head(2 * QK_NOPE), per_head(2 * QK_NOPE)]
    outs += [tok(D, act_dtype), tok(D, act_dtype), tok(D, act_dtype), tok(D, BF16), tok(D, BF16)]
    return pl.pallas_call(
        functools.partial(_inproj_kernel, absorb=absorb),
        grid=(G, Sg // tm),
        in_specs=in_specs, out_specs=[o[1] for o in outs], out_shape=[o[0] for o in outs],
        compiler_params=_params(("parallel", "parallel")),
        name="inproj",
    )(x3, mod3, *tabs, w["w_head"], w["w_r"], w["g_q"], w["g_kv"],
      w["w_uq_nope"], w["w_uq_pe"], w_uk)


def _mla_prompt_kernel(qf_ref, kf_ref, v_ref, wuv_ref, o_ref,
                       m_ref, l_ref, acc_ref, s_ref, *, tq):
    qi = pl.program_id(1)

    def rows_of(j):
        return pl.ds(pl.multiple_of(j * tq, tq), tq)

    def scores(head, j):
        return _dot_nt(qf_ref[0, head], kf_ref[0, head, rows_of(j), :])

    for hd in range(SCORE_LOOKAHEAD):
        s_ref[hd] = scores(hd, 0)

    def tile(j, first, last):
        v = v_ref[0, rows_of(j), :]
        col = lax.broadcasted_iota(jnp.int32, (tq, tq), 1)
        row = lax.broadcasted_iota(jnp.int32, (tq, tq), 0)
        if first:
            keep = col <= row + qi * tq
        elif last:
            keep = col <= row
        pending = [s_ref[hd] for hd in range(SCORE_LOOKAHEAD)]
        for head in range(MLA_HEADS):
            s = pending.pop(0)
            ahead = head + SCORE_LOOKAHEAD
            if ahead < MLA_HEADS:
                pending.append(scores(ahead, j))
            elif not last:
                s_ref[ahead - MLA_HEADS] = scores(ahead - MLA_HEADS, j + 1)
            if first or last:
                s = jnp.where(keep, s, NEG)
            m_cur = jnp.max(s, axis=-1, keepdims=True)
            m_new = jnp.broadcast_to(m_cur, (tq, LANES)) if first else jnp.maximum(m_ref[head], m_cur)
            p = jnp.exp2(s - jnp.concatenate([m_new] * (tq // LANES), axis=1))
            psum = p[:, 0:LANES]
            for c in range(1, tq // LANES):
                psum = psum + p[:, c * LANES:(c + 1) * LANES]
            pv = _dot(p.astype(BF16), v)
            if first:
                l_ref[head] = psum
                acc_ref[head] = pv
            else:
                a = jnp.exp2(m_ref[head] - m_new)
                l_ref[head] = a * l_ref[head] + psum
                acc_ref[head] = jnp.concatenate([a] * (KV_LORA // LANES), axis=1) * acc_ref[head] + pv
            m_ref[head] = m_new

    def body(j, carry):
        tile(j, False, False)
        return carry

    tile(0, True, False)
    lax.fori_loop(1, qi, body, 0)

    @pl.when(qi > 0)
    def _():
        tile(qi, False, True)

    heads = range(MLA_HEADS)
    inv = [1.0 / jnp.sum(l_ref[head], axis=-1, keepdims=True) for head in heads]
    lat = [(acc_ref[head] * inv[head]).astype(BF16) for head in heads]
    out = [_dot(lat[head], wuv_ref[head]) for head in heads]
    for head in heads:
        o_ref[0, :, head * V_HEAD:(head + 1) * V_HEAD] = out[head].astype(BF16)


def _mla_prompt(qf, kf, v, w_uv, *, tq):
    B, Hh, S, width = qf.shape
    assert tq % LANES == 0 and S >= 2 * tq
    return pl.pallas_call(
        functools.partial(_mla_prompt_kernel, tq=tq),
        grid=(B, S // tq),
        in_specs=[pl.BlockSpec((1, Hh, tq, width), lambda b, i: (b, 0, i, 0)),
                  pl.BlockSpec((1, Hh, S, width), lambda b, i: (b, 0, 0, 0)),
                  pl.BlockSpec((1, S, KV_LORA), lambda b, i: (b, 0, 0)),
                  _resident(w_uv.shape)],
        out_specs=pl.BlockSpec((1, tq, D_MODEL), lambda b, i: (b, i, 0)),
        out_shape=jax.ShapeDtypeStruct((B, S, D_MODEL), BF16),
        scratch_shapes=[pltpu.VMEM((Hh, tq, LANES), F32), pltpu.VMEM((Hh, tq, LANES), F32),
                        pltpu.VMEM((Hh, tq, KV_LORA), F32), pltpu.VMEM((SCORE_LOOKAHEAD, tq, tq), F32)],
        compiler_params=_params(("parallel", "parallel")),
        name="mla_prompt",
    )(qf, kf, v, w_uv)


def _mla_sample_kernel(pt_ref, q1_ref, q2_ref, cn_ref, kn_ref, ck_hbm, kp_hbm, o_ref,
                       ck_buf, kp_buf, sems, *, pages, groups, tokens):
    b = pl.program_id(0)
    n_b = pl.num_programs(0)
    n_pages = pages * groups

    def page_copies(bb, i):
        slot = lax.rem(bb, 2)
        page = pt_ref[bb, i]
        return (pltpu.make_async_copy(ck_hbm.at[0, page], ck_buf.at[slot, i], sems.at[0, slot]),
                pltpu.make_async_copy(kp_hbm.at[0, page], kp_buf.at[slot, i], sems.at[1, slot]))

    @pl.when(b == 0)
    def _():
        for i in range(n_pages):
            for cp in page_copies(b, i):
                cp.start()

    for i in range(n_pages):
        for cp in page_copies(b, i):
            cp.wait()

    def start_next_row(i):
        @pl.when(b + 1 < n_b)
        def _():
            for cp in page_copies(b + 1, i):
                cp.start()

    q1 = q1_ref[0]
    q2 = q2_ref[0]
    rows = q1.shape[0]
    slot = lax.rem(b, 2)

    def scores(g):
        cols = []
        for i in range(g * pages, (g + 1) * pages):
            cols.append(_dot_nt(q1, ck_buf[slot, i].astype(BF16)) + _dot(q2, kp_buf[slot, i].astype(BF16)))
            start_next_row(i)
        return cols

    def update(g, cols, m, l, acc):
        mx = cols[0]
        for c in cols[1:]:
            mx = jnp.maximum(mx, c)
        m_new = jnp.maximum(m, jnp.max(mx, axis=-1, keepdims=True))
        a = jnp.exp2(m - m_new)
        ps = [jnp.exp2(c - m_new) for c in cols]
        psum = ps[0]
        for p in ps[1:]:
            psum = psum + p
        pv = None
        for j in range(pages):
            part = _dot(ps[j].astype(BF16), ck_buf[slot, g * pages + j].astype(BF16))
            pv = part if pv is None else pv + part
        acc = jnp.concatenate([a] * (KV_LORA // LANES), axis=1) * acc + pv
        return m_new, a * l + psum, acc

    m = jnp.full((rows, LANES), -jnp.inf, F32)
    l = jnp.zeros((rows, LANES), F32)
    acc = jnp.zeros((rows, KV_LORA), F32)
    pending = scores(0)
    for g in range(groups):
        nxt = scores(g + 1) if g + 1 < groups else None
        m, l, acc = update(g, pending, m, l, acc)
        pending = nxt

    q1f = q1.astype(F32)
    q2f = q2.astype(F32)
    cn = cn_ref[0]
    kn = kn_ref[0]
    tok = lax.broadcasted_iota(jnp.int32, (rows, LANES), 0) & (tokens - 1)
    cols = []
    for t in range(tokens):
        sc = (jnp.sum(q1f * cn[t:t + 1, :], axis=-1, keepdims=True)
              + jnp.sum(q2f * kn[t:t + 1, :], axis=-1, keepdims=True))
        cols.append(jnp.where(tok >= t, sc, NEG))
    m_fin = m
    for sc in cols:
        m_fin = jnp.maximum(m_fin, sc)
    a2 = jnp.exp2(m - m_fin)
    l_fin = jnp.sum(a2 * l, axis=-1, keepdims=True)
    acc = jnp.concatenate([a2] * (KV_LORA // LANES), axis=1) * acc
    for t, sc in enumerate(cols):
        pt = jnp.exp2(sc - m_fin)
        l_fin = l_fin + pt[:, 0:1]
        acc = acc + jnp.concatenate([pt] * (KV_LORA // LANES), axis=1) * cn[t:t + 1, :]
    o_ref[0] = acc * (1.0 / l_fin)


def _mla_sample(page_table, q1, q2, cnew, knew, cache_ckv, cache_kpe_t, *, pages):
    DB, n_pages = page_table.shape
    rows = q1.shape[1]
    tokens = cnew.shape[1]
    assert n_pages % pages == 0 and tokens & (tokens - 1) == 0
    groups = n_pages // pages

    per_batch = lambda b, pt: (b, 0, 0)
    grid_spec = pltpu.PrefetchScalarGridSpec(
        num_scalar_prefetch=1,
        grid=(DB,),
        in_specs=[pl.BlockSpec((1, rows, KV_LORA), per_batch),
                  pl.BlockSpec((1, rows, QK_ROPE), per_batch),
                  pl.BlockSpec((1, tokens, KV_LORA), per_batch),
                  pl.BlockSpec((1, tokens, QK_ROPE), per_batch),
                  pl.BlockSpec(memory_space=pl.ANY),
                  pl.BlockSpec(memory_space=pl.ANY)],
        out_specs=pl.BlockSpec((1, rows, KV_LORA), per_batch),
        scratch_shapes=[pltpu.VMEM((2, n_pages, PAGE_SIZE, KV_LORA), F32),
                        pltpu.VMEM((2, n_pages, QK_ROPE, PAGE_SIZE), F32),
                        pltpu.SemaphoreType.DMA((2, 2))],
    )
    return pl.pallas_call(
        functools.partial(_mla_sample_kernel, pages=pages, groups=groups, tokens=tokens),
        grid_spec=grid_spec,
        out_shape=jax.ShapeDtypeStruct((DB, rows, KV_LORA), F32),
        compiler_params=_params(("arbitrary",)),
        name="mla_sample",
    )(page_table, q1, q2, cnew, knew, cache_ckv, cache_kpe_t)


def _retention_kernel(*refs, T, L, has_s0):
    C = RET_CHUNK
    padded = T < C
    if has_s0:
        q_ref, k_ref, v_ref, s0_ref, y_ref, s_ref = refs[:6]
        rest = refs[6:]
    else:
        q_ref, k_ref, v_ref, y_ref, s_ref = refs[:5]
        rest = refs[5:]
    heads = range(RET_HEADS)
    ri = lax.broadcasted_iota(jnp.int32, (C, C), 0).astype(F32)
    ci = lax.broadcasted_iota(jnp.int32, (C, C), 1).astype(F32)
    diff = ri - ci
    causal = diff >= 0.0
    log_gamma = [math.log(1.0 - 2.0 ** (-5.0 - h)) for h in heads]
    dmask = [jnp.where(causal, jnp.exp(jnp.where(causal, diff, 0.0) * lg), 0.0) for lg in log_gamma]
    q_decay = [jnp.exp((ri + 1.0) * lg) for lg in log_gamma]
    k_decay = [jnp.exp((L - 1.0 - ri) * lg) for lg in log_gamma]
    chunk_decay = [math.exp(L * lg) for lg in log_gamma]

    def col(h):
        return slice(h * RET_DK, (h + 1) * RET_DK)

    def chunk(load, store):
        q = [load(q_ref if not padded else rest[0], h).astype(BF16) for h in heads]
        k = [load(k_ref if not padded else rest[1], h) for h in heads]
        v = [load(v_ref if not padded else rest[2], h).astype(BF16) for h in heads]
        s = [s_ref[0, h] for h in heads]
        scores = [_dot_nt(q[h], k[h].astype(BF16)) for h in heads]
        cross = [_dot(q[h], s[h].astype(BF16)) for h in heads]
        kd_t = [(k[h].astype(F32) * k_decay[h]).T.astype(BF16) for h in heads]
        inner = [_dot((scores[h] * dmask[h]).astype(BF16), v[h]) for h in heads]
        upd = [_dot(kd_t[h], v[h]) for h in heads]
        for h in heads:
            y = inner[h] + cross[h] * q_decay[h]
            mu = jnp.mean(y, axis=-1, keepdims=True)
            yc = y - mu
            var = jnp.mean(yc * yc, axis=-1, keepdims=True)
            store(h, yc * lax.rsqrt(var + LN_EPS))
            s_ref[0, h] = chunk_decay[h] * s[h] + upd[h]

    for h in heads:
        s_ref[0, h] = s0_ref[0, h] if has_s0 else jnp.zeros((RET_DK, RET_DV), F32)
    if padded:
        for src, dst in zip((q_ref, k_ref, v_ref), rest):
            dst[...] = jnp.zeros(dst.shape, F32)
            dst[0:T, :] = src[0].astype(F32)

        def store(h, yn):
            y_ref[0, :, col(h)] = yn[0:T, :].astype(y_ref.dtype)
        chunk(lambda ref, h: ref[:, col(h)], store)
    else:
        def body(c, carry):
            rows = pl.ds(pl.multiple_of(c * C, C), C)

            def store(h, yn):
                y_ref[0, rows, col(h)] = yn.astype(y_ref.dtype)
            chunk(lambda ref, h: ref[0, rows, col(h)], store)
            return carry
        lax.fori_loop(0, T // C, body, 0)


def _retention(rq, rk, rv, s0, *, L, out_dtype):
    B, T, D = rq.shape
    has_s0 = s0 is not None
    padded = T < RET_CHUNK
    assert padded or T % RET_CHUNK == 0
    tok = pl.BlockSpec((1, T, D), lambda b: (b, 0, 0))
    state = pl.BlockSpec((1, RET_HEADS, RET_DK, RET_DV), lambda b: (b, 0, 0, 0))
    in_specs = [tok, tok, tok] + ([state] if has_s0 else [])
    args = (rq, rk, rv) + ((s0,) if has_s0 else ())
    return pl.pallas_call(
        functools.partial(_retention_kernel, T=T, L=L, has_s0=has_s0),
        grid=(B,),
        in_specs=in_specs,
        out_specs=[tok, state],
        out_shape=[jax.ShapeDtypeStruct((B, T, D), out_dtype),
                   jax.ShapeDtypeStruct((B, RET_HEADS, RET_DK, RET_DV), F32)],
        scratch_shapes=[pltpu.VMEM((RET_CHUNK, D), F32)] * 3 if padded else [],
        compiler_params=_params(("parallel",)),
        name="retention",
    )(*args)


def _post_kernel(*refs, apply_uv):
    if apply_uv:
        (x_ref, att_ref, y_ref, ga_ref, gr_ref, mod_ref, wuv_ref, wo_ref, wup_ref, wdn_ref,
         g1_ref, b1_ref, g2_ref, b2_ref, o_ref) = refs
    else:
        (x_ref, att_ref, y_ref, ga_ref, gr_ref, mod_ref, wo_ref, wup_ref, wdn_ref,
         g1_ref, b1_ref, g2_ref, b2_ref, o_ref) = refs
    D = D_MODEL
    x = x_ref[0]
    if apply_uv:
        o_a = jnp.concatenate([_dot(att_ref[0, head].astype(BF16), wuv_ref[head])
                               for head in range(MLA_HEADS)], axis=1)
    else:
        o_a = att_ref[0].astype(F32)
    mixv = ga_ref[0].astype(F32) * o_a + gr_ref[0].astype(F32) * y_ref[0].astype(F32)
    mix = _dot(mixv.astype(BF16), wo_ref[...])
    g1 = mod_ref[0, :, 2 * D:3 * D]
    sh2 = mod_ref[0, :, 3 * D:4 * D]
    sc2 = mod_ref[0, :, 4 * D:5 * D]
    g2 = mod_ref[0, :, 5 * D:6 * D]
    x1 = _layer_norm(ALPHA * x + (1.0 + g1) * mix, g1_ref[...], b1_ref[...])
    h2 = (x1 * (1.0 + sc2) + sh2).astype(BF16)
    m = None
    for c in range(D_FF // D):
        u = jnp.maximum(_dot(h2, wup_ref[:, c * D:(c + 1) * D]), 0.0)
        part = _dot((u * u).astype(BF16), wdn_ref[c * D:(c + 1) * D, :])
        m = part if m is None else m + part
    o_ref[0] = _layer_norm(ALPHA * x1 + (1.0 + g2) * m, g2_ref[...], b2_ref[...])


def _post(x3, att, y3, ga, gr, mod3, w, *, tm, apply_uv):
    G, Sg, D = x3.shape
    R = mod3.shape[1]
    row = lambda g, i: (g, i, 0)
    act = pl.BlockSpec((1, tm, D), row)
    if apply_uv:
        att_spec = pl.BlockSpec((1, MLA_HEADS, tm, KV_LORA), lambda g, i: (g, 0, i, 0))
    else:
        att_spec = act
    in_specs = [act, att_spec, act, act, act, pl.BlockSpec((1, R, 6 * D), lambda g, i: (g, 0, 0))]
    args = [x3, att, y3, ga, gr, mod3]
    names = (["w_uv"] if apply_uv else []) + ["w_o", "w_up", "w_down", "ln1_g", "ln1_b", "ln2_g", "ln2_b"]
    for n in names:
        in_specs.append(_resident(w[n].shape))
        args.append(w[n])
    return pl.pallas_call(
        functools.partial(_post_kernel, apply_uv=apply_uv),
        grid=(G, Sg // tm),
        in_specs=in_specs,
        out_specs=act,
        out_shape=jax.ShapeDtypeStruct((G, Sg, D), F32),
        compiler_params=_params(("parallel", "parallel")),
        name="post",
    )(*args)


def _rope_tables(pos, dim):
    half = dim // 2
    inv = ROPE_BASE ** (-jnp.arange(half, dtype=F32) / half)
    ang = pos.astype(F32)[:, None] * inv[None, :]
    cos, sin = jnp.cos(ang), jnp.sin(ang)
    reps = LANES // dim
    return (jnp.tile(jnp.concatenate([cos, cos], -1), (1, reps)),
            jnp.tile(jnp.concatenate([-sin, sin], -1), (1, reps)))


def kernel(x_prompt, x_sample, c_prompt, c_sample, cache_ckv, cache_kpe, state_ret, page_table,
           w_ada, b_ada, w_in, g_qnorm, g_kvnorm, w_uq, w_uk, w_uv, w_o,
           ln1_g, ln1_b, w_up, w_down, ln2_g, ln2_b):
    B, S, D = x_prompt.shape
    DB, T, _ = x_sample.shape
    n_pages = page_table.shape[1]
    past_len = n_pages * PAGE_SIZE
    assert DEPTH == 1 and w_in.shape[0] == 1
    l = 0
    n_head = Q_LORA + KV_LORA + QK_ROPE

    w_kr = w_in[l][:, Q_LORA + KV_LORA:n_head]
    w = {
        "w_head": jnp.concatenate([w_in[l][:, :n_head], w_kr], axis=1).astype(BF16),
        "w_r": w_in[l][:, n_head:].astype(BF16),
        "g_q": g_qnorm[l][None, :], "g_kv": g_kvnorm[l][None, :],
        "w_uq_nope": w_uq[l][:, :, :QK_NOPE].reshape(Q_LORA, MLA_HEADS * QK_NOPE).astype(BF16),
        "w_uq_pe": w_uq[l][:, :, QK_NOPE:].reshape(Q_LORA, MLA_HEADS * QK_ROPE).astype(BF16),
        "w_ukT": jnp.transpose(w_uk[l], (1, 2, 0)).astype(BF16),
        "w_uk_flat": w_uk[l].reshape(KV_LORA, MLA_HEADS * QK_NOPE).astype(BF16),
        "w_uv": jnp.transpose(w_uv[l], (1, 0, 2)).astype(BF16),
        "w_o": w_o[l].astype(BF16), "w_up": w_up[l].astype(BF16), "w_down": w_down[l].astype(BF16),
        "ln1_g": ln1_g[l][None, :], "ln1_b": ln1_b[l][None, :],
        "ln2_g": ln2_g[l][None, :], "ln2_b": ln2_b[l][None, :],
    }

    n_c = B + DB
    pad = (-n_c) % 16
    c_all = jnp.concatenate([c_prompt, c_sample, jnp.zeros((pad, D), F32)], axis=0)
    ada = _ada(c_all, w_ada[l], b_ada[l][None, :])
    mod_p = ada[:B].reshape(B, 1, 6 * D)
    mod_s = jnp.repeat(ada[B:n_c], T, axis=0).reshape(1, DB * T, 6 * D)

    pos_p = jnp.arange(S)
    tabs_p = _rope_tables(pos_p, QK_ROPE) + _rope_tables(pos_p, RET_DK)
    (ckv_p, kpe_p, kc_p, qf_p, kf_p, rq_p, rk_p, rv_p, ga_p, gr_p) = _inproj(
        x_prompt, mod_p, tabs_p, w, tm=256, act_dtype=BF16, absorb=False)
    oa_p = _mla_prompt(qf_p, kf_p, kc_p, w["w_uv"], tq=256)
    y_p, ret_p = _retention(rq_p, rk_p, rv_p, None, L=min(RET_CHUNK, S), out_dtype=BF16)
    out_p = _post(x_prompt, oa_p, y_p, ga_p, gr_p, mod_p, w, tm=512, apply_uv=False)

    n_s = DB * T
    pos_s = jnp.tile(past_len + jnp.arange(T), DB)
    tabs_s = _rope_tables(pos_s, QK_ROPE) + _rope_tables(pos_s, RET_DK)
    xs = x_sample.reshape(1, n_s, D)
    (ckv_s, kpe_s, _, _, qlat_s, qpe_s, rq_s, rk_s, rv_s, ga_s, gr_s) = _inproj(
        xs, mod_s, tabs_s, w, tm=n_s, act_dtype=F32, absorb=True)

    def to_batch_rows(q):
        d = q.shape[-1]
        return q[0].reshape(MLA_HEADS, DB, T, d).transpose(1, 0, 2, 3).reshape(DB, MLA_HEADS * T, d)

    olat_s = _mla_sample(page_table, to_batch_rows(qlat_s), to_batch_rows(qpe_s),
                         ckv_s.reshape(DB, T, KV_LORA), kpe_s.reshape(DB, T, QK_ROPE),
                         cache_ckv, jnp.swapaxes(cache_kpe, 2, 3), pages=16)
    olat_s = olat_s.reshape(DB, MLA_HEADS, T, KV_LORA).transpose(1, 0, 2, 3).reshape(1, MLA_HEADS, n_s, KV_LORA)
    y_s, ret_s = _retention(rq_s.reshape(DB, T, D), rk_s.reshape(DB, T, D), rv_s.reshape(DB, T, D),
                            state_ret[l], L=T, out_dtype=F32)
    out_s = _post(xs, olat_s, y_s.reshape(1, n_s, D), ga_s, gr_s, mod_s, w, tm=n_s, apply_uv=True)

    return (out_p, out_s.reshape(DB, T, D),
            ckv_p[None], kpe_p[None], ret_p[None],
            ckv_s.reshape(1, DB, T, KV_LORA), kpe_s.reshape(1, DB, T, QK_ROPE), ret_s[None])
```

```python
import functools
import math

import jax
import jax.numpy as jnp
from jax import lax
from jax.experimental import pallas as pl
from jax.experimental.pallas import tpu as pltpu

D_MODEL = 1024
DEPTH = 1
PAGE_SIZE = 128
MLA_HEADS = 8
Q_LORA = 384
KV_LORA = 256
QK_NOPE = 128
QK_ROPE = 64
V_HEAD = D_MODEL // MLA_HEADS
MLA_SCALE = (QK_NOPE + QK_ROPE) ** -0.5
LOG2E = 1.4426950408889634
Q_SCALE = MLA_SCALE * LOG2E
RET_HEADS = 8
RET_DK = 128
RET_DV = D_MODEL // RET_HEADS
RET_CHUNK = 128
D_FF = 4 * D_MODEL
ROPE_BASE = 10000.0
LN_EPS = 1e-5
RMS_EPS = 1e-6
NEG = -1e30
ALPHA = (2.0 * DEPTH) ** 0.25

SCORE_LOOKAHEAD = 3
POST_SPLIT = 2
LANES = 128
V7X_VMEM_BYTES = 64 * 1024 * 1024
VMEM_LIMIT = V7X_VMEM_BYTES - 8 * 1024 * 1024

F32 = jnp.float32
BF16 = jnp.bfloat16


def _params(sem):
    return pltpu.CompilerParams(dimension_semantics=sem, vmem_limit_bytes=VMEM_LIMIT)


def _resident(shape):
    nd = len(shape)
    return pl.BlockSpec(shape, lambda *_: (0,) * nd, pipeline_mode=pl.Buffered(1))


def _dot(a, b):
    return jnp.dot(a, b, preferred_element_type=F32)


def _dot_nt(a, b):
    return lax.dot_general(a, b, (((1,), (1,)), ((), ())), preferred_element_type=F32)


def _rope(x, cos, sin_signed, half):
    lanes = x.shape[-1]
    if 2 * half == lanes:
        partner = pltpu.roll(x, half, axis=1)
    else:
        lane = lax.broadcasted_iota(jnp.int32, x.shape, 1)
        first = (lane & (2 * half - 1)) < half
        partner = jnp.where(first, pltpu.roll(x, lanes - half, axis=1), pltpu.roll(x, half, axis=1))
    return x * cos + partner * sin_signed


def _layer_norm(v, g, b):
    mu = jnp.mean(v, axis=-1, keepdims=True)
    vc = v - mu
    var = jnp.mean(vc * vc, axis=-1, keepdims=True)
    return vc * lax.rsqrt(var + LN_EPS) * g + b


def _ada_kernel(c_ref, w_ref, b_ref, o_ref):
    c = c_ref[...]
    s = (c * jax.nn.sigmoid(c)).astype(BF16)
    o_ref[...] = _dot(s, w_ref[...].astype(BF16)) + b_ref[...]


def _ada(c_all, w_ada, b_ada):
    rows = c_all.shape[0]
    tn = D_MODEL
    return pl.pallas_call(
        _ada_kernel,
        grid=(6 * D_MODEL // tn,),
        in_specs=[pl.BlockSpec((rows, D_MODEL), lambda j: (0, 0)),
                  pl.BlockSpec((D_MODEL, tn), lambda j: (0, j)),
                  pl.BlockSpec((1, tn), lambda j: (0, j))],
        out_specs=pl.BlockSpec((rows, tn), lambda j: (0, j)),
        out_shape=jax.ShapeDtypeStruct((rows, 6 * D_MODEL), F32),
        compiler_params=_params(("parallel",)),
        name="ada",
    )(c_all, w_ada, b_ada)


def _inproj_kernel(x_ref, mod_ref, cosa_ref, sina_ref, cosb_ref, sinb_ref,
                   wh_ref, wr_ref, gq_ref, gkv_ref, wqn_ref, wqp_ref, wuk_ref,
                   ckv_ref, kpe_ref, kc_ref, *refs, absorb):
    if absorb:
        kp_ref, qlat_ref, qpe_ref, rq_ref, rk_ref, rv_ref, ga_ref, gr_ref = refs
    else:
        qf_ref, kf_ref, rq_ref, rk_ref, rv_ref, ga_ref, gr_ref = refs
    D = D_MODEL
    x = x_ref[0]
    sh1 = mod_ref[0, :, 0:D]
    sc1 = mod_ref[0, :, D:2 * D]
    h = (x * (1.0 + sc1) + sh1).astype(BF16)

    yh = _dot(h, wh_ref[...])
    pq = yh[:, 0:Q_LORA]
    pkv = yh[:, Q_LORA:Q_LORA + KV_LORA]
    pkr = yh[:, Q_LORA + KV_LORA:Q_LORA + KV_LORA + LANES]
    qn = (pq * lax.rsqrt(jnp.mean(pq * pq, axis=-1, keepdims=True) + RMS_EPS) * gq_ref[...]).astype(BF16)
    ckv = pkv * lax.rsqrt(jnp.mean(pkv * pkv, axis=-1, keepdims=True) + RMS_EPS) * gkv_ref[...]
    ckv_ref[0] = ckv
    kc_ref[0] = ckv.astype(BF16)
    cosa = cosa_ref[...]
    sina = sina_ref[...]
    kpe2 = _rope(pkr, cosa, sina, QK_ROPE // 2)
    kpe_ref[0] = kpe2[:, 0:QK_ROPE]

    qpe = _dot(qn, wqp_ref[...])
    qnope = _dot(qn, wqn_ref[...])
    pe_blocks = [_rope(qpe[:, j * LANES:(j + 1) * LANES], cosa, sina, QK_ROPE // 2) * Q_SCALE
                 for j in range(MLA_HEADS * QK_ROPE // LANES)]
    if absorb:
        kp_ref[0] = kpe2[:, 0:QK_ROPE].astype(BF16)
        for head in range(MLA_HEADS):
            off = (head % 2) * QK_ROPE
            qpe_ref[0, head] = pe_blocks[head // 2][:, off:off + QK_ROPE].astype(BF16)
            qh = qnope[:, head * QK_NOPE:(head + 1) * QK_NOPE].astype(BF16)
            qlat_ref[0, head] = (_dot(qh, wuk_ref[head]) * Q_SCALE).astype(BF16)
    else:
        lane = lax.broadcasted_iota(jnp.int32, kpe2.shape, 1)
        kpe_pad = jnp.where(lane < QK_ROPE, kpe2, 0.0).astype(BF16)
        knope = _dot(ckv.astype(BF16), wuk_ref[...])
        for head in range(MLA_HEADS):
            sl = slice(head * QK_NOPE, (head + 1) * QK_NOPE)
            kf_ref[0, head, :, 0:QK_NOPE] = knope[:, sl].astype(BF16)
            kf_ref[0, head, :, QK_NOPE:2 * QK_NOPE] = kpe_pad
            qf_ref[0, head, :, 0:QK_NOPE] = (qnope[:, sl] * Q_SCALE).astype(BF16)
            blk = pe_blocks[head // 2]
            if head % 2:
                blk = pltpu.roll(blk, QK_ROPE, axis=1)
            qf_ref[0, head, :, QK_NOPE:2 * QK_NOPE] = blk.astype(BF16)

    cosb = cosb_ref[...]
    sinb = sinb_ref[...]
    rq = _dot(h, wr_ref[:, 0:D])
    rk = _dot(h, wr_ref[:, D:2 * D])
    for head in range(RET_HEADS):
        sl = slice(head * RET_DK, (head + 1) * RET_DK)
        rq_ref[0, :, sl] = _rope(rq[:, sl], cosb, sinb, RET_DK // 2).astype(rq_ref.dtype)
        rk_h = _rope(rk[:, sl], cosb, sinb, RET_DK // 2) * (RET_DK ** -0.5)
        if absorb:
            rk_ref[0, :, sl] = rk_h.astype(rk_ref.dtype)
        else:
            rk_ref[0, sl, :] = rk_h.T.astype(rk_ref.dtype)
    rv_ref[0] = _dot(h, wr_ref[:, 2 * D:3 * D]).astype(rv_ref.dtype)
    rg = _dot(h, wr_ref[:, 3 * D:4 * D])
    ga_ref[0] = jax.nn.sigmoid(_dot(h, wr_ref[:, 4 * D:5 * D])).astype(BF16)
    gr = jax.nn.sigmoid(_dot(h, wr_ref[:, 5 * D:6 * D]))
    gr_ref[0] = (gr * (rg * jax.nn.sigmoid(rg))).astype(BF16)


def _inproj(x3, mod3, tabs, w, *, tm, act_dtype, absorb):
    G, Sg, D = x3.shape
    R = mod3.shape[1]
    assert Sg % tm == 0 and R in (1, tm) and (R == 1 or Sg == tm)
    row = lambda g, i: (g, i, 0)
    head_row = lambda g, i: (g, 0, i, 0)
    tab = pl.BlockSpec((tm, LANES), lambda g, i: (i, 0))
    w_uk = w["w_ukT"] if absorb else w["w_uk_flat"]
    in_specs = [
        pl.BlockSpec((1, tm, D), row),
        pl.BlockSpec((1, R, 2 * D), lambda g, i: (g, 0, 0)),
        tab, tab, tab, tab,
        _resident(w["w_head"].shape), _resident(w["w_r"].shape),
        _resident(w["g_q"].shape), _resident(w["g_kv"].shape),
        _resident(w["w_uq_nope"].shape), _resident(w["w_uq_pe"].shape), _resident(w_uk.shape),
    ]

    def tok(width, dtype):
        return jax.ShapeDtypeStruct((G, Sg, width), dtype), pl.BlockSpec((1, tm, width), row)

    def per_head(width):
        return (jax.ShapeDtypeStruct((G, MLA_HEADS, Sg, width), BF16),
                pl.BlockSpec((1, MLA_HEADS, tm, width), head_row))

    outs = [tok(KV_LORA, F32), tok(QK_ROPE, F32), tok(KV_LORA, BF16)]
    if absorb:
        outs += [tok(QK_ROPE, BF16), per_head(KV_LORA), per_head(QK_ROPE)]
    else:
        outs += [per_head(2 * QK_NOPE), per_head(2 * QK_NOPE)]
    ret_k = tok(D, act_dtype) if absorb else (jax.ShapeDtypeStruct((G, D, Sg), act_dtype),
                                              pl.BlockSpec((1, D, tm), lambda g, i: (g, 0, i)))
    outs += [tok(D, act_dtype), ret_k, tok(D, act_dtype), tok(D, BF16), tok(D, BF16)]
    return pl.pallas_call(
        functools.partial(_inproj_kernel, absorb=absorb),
        grid=(G, Sg // tm),
        in_specs=in_specs, out_specs=[o[1] for o in outs], out_shape=[o[0] for o in outs],
        compiler_params=_params(("parallel", "parallel")),
        name="inproj",
    )(x3, mod3, *tabs, w["w_head"], w["w_r"], w["g_q"], w["g_kv"],
      w["w_uq_nope"], w["w_uq_pe"], w_uk)


def _mla_prompt_kernel(qf_ref, kf_ref, v_ref, wuv_ref, o_ref,
                       m_ref, l_ref, acc_ref, s_ref, *, tq):
    qi = pl.program_id(1)

    def rows_of(j):
        return pl.ds(pl.multiple_of(j * tq, tq), tq)

    def scores(head, j):
        return _dot_nt(qf_ref[0, head], kf_ref[0, head, rows_of(j), :])

    for hd in range(SCORE_LOOKAHEAD):
        s_ref[hd] = scores(hd, 0)

    def tile(j, first, last):
        v = v_ref[0, rows_of(j), :]
        col = lax.broadcasted_iota(jnp.int32, (tq, tq), 1)
        row = lax.broadcasted_iota(jnp.int32, (tq, tq), 0)
        if first:
            keep = col <= row + qi * tq
        elif last:
            keep = col <= row
        pending = [s_ref[hd] for hd in range(SCORE_LOOKAHEAD)]
        for head in range(MLA_HEADS):
            s = pending.pop(0)
            ahead = head + SCORE_LOOKAHEAD
            if ahead < MLA_HEADS:
                pending.append(scores(ahead, j))
            elif not last:
                s_ref[ahead - MLA_HEADS] = scores(ahead - MLA_HEADS, j + 1)
            if first or last:
                s = jnp.where(keep, s, NEG)
            m_cur = jnp.max(s, axis=-1, keepdims=True)
            m_new = jnp.broadcast_to(m_cur, (tq, LANES)) if first else jnp.maximum(m_ref[head], m_cur)
            p = jnp.exp2(s - jnp.concatenate([m_new] * (tq // LANES), axis=1))
            psum = p[:, 0:LANES]
            for c in range(1, tq // LANES):
                psum = psum + p[:, c * LANES:(c + 1) * LANES]
            pv = _dot(p.astype(BF16), v)
            if first:
                l_ref[head] = psum
                acc_ref[head] = pv
            else:
                a = jnp.exp2(m_ref[head] - m_new)
                l_ref[head] = a * l_ref[head] + psum
                acc_ref[head] = jnp.concatenate([a] * (KV_LORA // LANES), axis=1) * acc_ref[head] + pv
            m_ref[head] = m_new

    def body(j, carry):
        tile(j, False, False)
        return carry

    tile(0, True, False)
    lax.fori_loop(1, qi, body, 0)

    @pl.when(qi > 0)
    def _():
        tile(qi, False, True)

    heads = range(MLA_HEADS)
    inv = [1.0 / jnp.sum(l_ref[head], axis=-1, keepdims=True) for head in heads]
    lat = [(acc_ref[head] * inv[head]).astype(BF16) for head in heads]
    out = [_dot(lat[head], wuv_ref[head]) for head in heads]
    for head in heads:
        o_ref[0, :, head * V_HEAD:(head + 1) * V_HEAD] = out[head].astype(BF16)


def _mla_prompt(qf, kf, v, w_uv, *, tq):
    B, Hh, S, width = qf.shape
    assert tq % LANES == 0 and S >= 2 * tq
    return pl.pallas_call(
        functools.partial(_mla_prompt_kernel, tq=tq),
        grid=(B, S // tq),
        in_specs=[pl.BlockSpec((1, Hh, tq, width), lambda b, i: (b, 0, i, 0)),
                  pl.BlockSpec((1, Hh, S, width), lambda b, i: (b, 0, 0, 0)),
                  pl.BlockSpec((1, S, KV_LORA), lambda b, i: (b, 0, 0)),
                  _resident(w_uv.shape)],
        out_specs=pl.BlockSpec((1, tq, D_MODEL), lambda b, i: (b, i, 0)),
        out_shape=jax.ShapeDtypeStruct((B, S, D_MODEL), BF16),
        scratch_shapes=[pltpu.VMEM((Hh, tq, LANES), F32), pltpu.VMEM((Hh, tq, LANES), F32),
                        pltpu.VMEM((Hh, tq, KV_LORA), F32), pltpu.VMEM((SCORE_LOOKAHEAD, tq, tq), F32)],
        compiler_params=_params(("parallel", "parallel")),
        name="mla_prompt",
    )(qf, kf, v, w_uv)


def _mla_sample_kernel(pt_ref, q1_ref, q2_ref, cn_ref, kn_ref, ck_hbm, kp_hbm, o_ref,
                       ck_buf, kp_buf, sems, *, pages, groups, tokens):
    b = pl.program_id(0)
    n_b = pl.num_programs(0)
    n_pages = pages * groups

    def page_copies(bb, i):
        slot = lax.rem(bb, 2)
        page = pt_ref[bb, i]
        return (pltpu.make_async_copy(ck_hbm.at[0, page], ck_buf.at[slot, i], sems.at[0, slot]),
                pltpu.make_async_copy(kp_hbm.at[0, page], kp_buf.at[slot, i], sems.at[1, slot]))

    @pl.when(b == 0)
    def _():
        for i in range(n_pages):
            for cp in page_copies(b, i):
                cp.start()

    for i in range(n_pages):
        for cp in page_copies(b, i):
            cp.wait()

    def start_next_row(i):
        @pl.when(b + 1 < n_b)
        def _():
            for cp in page_copies(b + 1, i):
                cp.start()

    q1 = q1_ref[0]
    q2 = q2_ref[0]
    rows = q1.shape[0]
    slot = lax.rem(b, 2)

    def scores(g):
        cols = []
        for i in range(g * pages, (g + 1) * pages):
            cols.append(_dot_nt(q1, ck_buf[slot, i].astype(BF16)) + _dot(q2, kp_buf[slot, i].astype(BF16)))
            start_next_row(i)
        return cols

    def update(g, cols, m, l, acc):
        mx = cols[0]
        for c in cols[1:]:
            mx = jnp.maximum(mx, c)
        m_new = jnp.maximum(m, jnp.max(mx, axis=-1, keepdims=True))
        a = jnp.exp2(m - m_new)
        ps = [jnp.exp2(c - m_new) for c in cols]
        psum = ps[0]
        for p in ps[1:]:
            psum = psum + p
        pv = None
        for j in range(pages):
            part = _dot(ps[j].astype(BF16), ck_buf[slot, g * pages + j].astype(BF16))
            pv = part if pv is None else pv + part
        acc = jnp.concatenate([a] * (KV_LORA // LANES), axis=1) * acc + pv
        return m_new, a * l + psum, acc

    m = jnp.full((rows, LANES), -jnp.inf, F32)
    l = jnp.zeros((rows, LANES), F32)
    acc = jnp.zeros((rows, KV_LORA), F32)
    pending = scores(0)
    for g in range(groups):
        nxt = scores(g + 1) if g + 1 < groups else None
        m, l, acc = update(g, pending, m, l, acc)
        pending = nxt

    q1f = q1.astype(F32)
    q2f = q2.astype(F32)
    cn = cn_ref[0]
    kn = kn_ref[0]
    tok = lax.broadcasted_iota(jnp.int32, (rows, LANES), 0) & (tokens - 1)
    cols = []
    for t in range(tokens):
        sc = (jnp.sum(q1f * cn[t:t + 1, :], axis=-1, keepdims=True)
              + jnp.sum(q2f * kn[t:t + 1, :], axis=-1, keepdims=True))
        cols.append(jnp.where(tok >= t, sc, NEG))
    m_fin = m
    for sc in cols:
        m_fin = jnp.maximum(m_fin, sc)
    a2 = jnp.exp2(m - m_fin)
    l_fin = jnp.sum(a2 * l, axis=-1, keepdims=True)
    acc = jnp.concatenate([a2] * (KV_LORA // LANES), axis=1) * acc
    for t, sc in enumerate(cols):
        pt = jnp.exp2(sc - m_fin)
        l_fin = l_fin + pt[:, 0:1]
        acc = acc + jnp.concatenate([pt] * (KV_LORA // LANES), axis=1) * cn[t:t + 1, :]
    o_ref[0] = acc * (1.0 / l_fin)


def _mla_sample(page_table, q1, q2, cnew, knew, cache_ckv, cache_kpe_t, *, pages):
    DB, n_pages = page_table.shape
    rows = q1.shape[1]
    tokens = cnew.shape[1]
    assert n_pages % pages == 0 and tokens & (tokens - 1) == 0
    groups = n_pages // pages

    per_batch = lambda b, pt: (b, 0, 0)
    grid_spec = pltpu.PrefetchScalarGridSpec(
        num_scalar_prefetch=1,
        grid=(DB,),
        in_specs=[pl.BlockSpec((1, rows, KV_LORA), per_batch),
                  pl.BlockSpec((1, rows, QK_ROPE), per_batch),
                  pl.BlockSpec((1, tokens, KV_LORA), per_batch),
                  pl.BlockSpec((1, tokens, QK_ROPE), per_batch),
                  pl.BlockSpec(memory_space=pl.ANY),
                  pl.BlockSpec(memory_space=pl.ANY)],
        out_specs=pl.BlockSpec((1, rows, KV_LORA), per_batch),
        scratch_shapes=[pltpu.VMEM((2, n_pages, PAGE_SIZE, KV_LORA), F32),
                        pltpu.VMEM((2, n_pages, QK_ROPE, PAGE_SIZE), F32),
                        pltpu.SemaphoreType.DMA((2, 2))],
    )
    return pl.pallas_call(
        functools.partial(_mla_sample_kernel, pages=pages, groups=groups, tokens=tokens),
        grid_spec=grid_spec,
        out_shape=jax.ShapeDtypeStruct((DB, rows, KV_LORA), F32),
        compiler_params=_params(("arbitrary",)),
        name="mla_sample",
    )(page_table, q1, q2, cnew, knew, cache_ckv, cache_kpe_t)


def _retention_kernel(*refs, T, L, has_s0):
    C = RET_CHUNK
    padded = T < C
    if has_s0:
        q_ref, k_ref, v_ref, s0_ref, y_ref, s_ref = refs[:6]
        rest = refs[6:]
    else:
        q_ref, k_ref, v_ref, y_ref, s_ref = refs[:5]
        rest = refs[5:]
    heads = range(RET_HEADS)
    ri = lax.broadcasted_iota(jnp.int32, (C, C), 0).astype(F32)
    ci = lax.broadcasted_iota(jnp.int32, (C, C), 1).astype(F32)
    diff = ri - ci
    causal = diff >= 0.0
    log_gamma = [math.log(1.0 - 2.0 ** (-5.0 - h)) for h in heads]
    dmask = [jnp.where(causal, jnp.exp(jnp.where(causal, diff, 0.0) * lg), 0.0) for lg in log_gamma]
    q_decay = [jnp.exp((ri + 1.0) * lg) for lg in log_gamma]
    k_decay = [jnp.exp((L - 1.0 - ci) * lg) for lg in log_gamma]
    chunk_decay = [math.exp(L * lg) for lg in log_gamma]

    def col(h):
        return slice(h * RET_DK, (h + 1) * RET_DK)

    def chunk(q, kt, v, store):
        s = [s_ref[0, h] for h in heads]
        scores = [_dot(q[h], kt[h].astype(BF16)) for h in heads]
        cross = [_dot(q[h], s[h].astype(BF16)) for h in heads]
        kd_t = [(kt[h].astype(F32) * k_decay[h]).astype(BF16) for h in heads]
        inner = [_dot((scores[h] * dmask[h]).astype(BF16), v[h]) for h in heads]
        upd = [_dot(kd_t[h], v[h]) for h in heads]
        for h in heads:
            store(h, inner[h] + cross[h] * q_decay[h])
            s_ref[0, h] = chunk_decay[h] * s[h] + upd[h]

    for h in heads:
        s_ref[0, h] = s0_ref[0, h] if has_s0 else jnp.zeros((RET_DK, RET_DV), F32)
    if padded:
        for src, dst in zip((q_ref, k_ref, v_ref), rest):
            dst[...] = jnp.zeros(dst.shape, F32)
            dst[0:T, :] = src[0].astype(F32)
        qp_ref, kp_ref, vp_ref = rest

        def store(h, y):
            y_ref[0, :, col(h)] = y[0:T, :].astype(y_ref.dtype)
        chunk([qp_ref[:, col(h)].astype(BF16) for h in heads],
              [kp_ref[:, col(h)].T for h in heads],
              [vp_ref[:, col(h)].astype(BF16) for h in heads], store)
    else:
        def body(c, carry):
            rows = pl.ds(pl.multiple_of(c * C, C), C)

            def store(h, y):
                y_ref[0, rows, col(h)] = y.astype(y_ref.dtype)
            chunk([q_ref[0, rows, col(h)] for h in heads],
                  [k_ref[0, col(h), rows] for h in heads],
                  [v_ref[0, rows, col(h)] for h in heads], store)
            return carry
        lax.fori_loop(0, T // C, body, 0)


def _retention(rq, rk, rv, s0, *, L, out_dtype):
    B, T, D = rq.shape
    has_s0 = s0 is not None
    padded = T < RET_CHUNK
    assert padded or T % RET_CHUNK == 0
    tok = pl.BlockSpec((1, T, D), lambda b: (b, 0, 0))
    key = tok if padded else pl.BlockSpec((1, D, T), lambda b: (b, 0, 0))
    state = pl.BlockSpec((1, RET_HEADS, RET_DK, RET_DV), lambda b: (b, 0, 0, 0))
    in_specs = [tok, key, tok] + ([state] if has_s0 else [])
    args = (rq, rk, rv) + ((s0,) if has_s0 else ())
    return pl.pallas_call(
        functools.partial(_retention_kernel, T=T, L=L, has_s0=has_s0),
        grid=(B,),
        in_specs=in_specs,
        out_specs=[tok, state],
        out_shape=[jax.ShapeDtypeStruct((B, T, D), out_dtype),
                   jax.ShapeDtypeStruct((B, RET_HEADS, RET_DK, RET_DV), F32)],
        scratch_shapes=[pltpu.VMEM((RET_CHUNK, D), F32)] * 3 if padded else [],
        compiler_params=_params(("parallel",)),
        name="retention",
    )(*args)


def _post_kernel(*refs, apply_uv):
    if apply_uv:
        (x_ref, att_ref, y_ref, ga_ref, gr_ref, mod_ref, wuv_ref, wo_ref, wup_ref, wdn_ref,
         g1_ref, b1_ref, g2_ref, b2_ref, o_ref) = refs
    else:
        (x_ref, att_ref, y_ref, ga_ref, gr_ref, mod_ref, wo_ref, wup_ref, wdn_ref,
         g1_ref, b1_ref, g2_ref, b2_ref, o_ref) = refs
    D = D_MODEL
    tm = x_ref.shape[1]
    n_sub = POST_SPLIT if tm % (POST_SPLIT * 16) == 0 else 1
    subs = [slice(i * (tm // n_sub), (i + 1) * (tm // n_sub)) for i in range(n_sub)]

    def mod(rows, k):
        cols = slice(k * D, (k + 1) * D)
        return mod_ref[0, :, cols] if mod_ref.shape[1] == 1 else mod_ref[0, rows, cols]

    def mix_in(rows):
        if apply_uv:
            o_a = jnp.concatenate([_dot(att_ref[0, head, rows, :].astype(BF16), wuv_ref[head])
                                   for head in range(MLA_HEADS)], axis=1)
        else:
            o_a = att_ref[0, rows, :].astype(F32)
        y = y_ref[0, rows, :].astype(F32)
        normed = []
        for head in range(RET_HEADS):
            yh = y[:, head * RET_DV:(head + 1) * RET_DV]
            yc = yh - jnp.mean(yh, axis=-1, keepdims=True)
            normed.append(yc * lax.rsqrt(jnp.mean(yc * yc, axis=-1, keepdims=True) + LN_EPS))
        mixv = (ga_ref[0, rows, :].astype(F32) * o_a
                + gr_ref[0, rows, :].astype(F32) * jnp.concatenate(normed, axis=1))
        return mixv.astype(BF16)

    mixv = [mix_in(rows) for rows in subs]
    mix = [_dot(mv, wo_ref[...]) for mv in mixv]
    x1 = [_layer_norm(ALPHA * x_ref[0, rows, :] + (1.0 + mod(rows, 2)) * mx, g1_ref[...], b1_ref[...])
          for rows, mx in zip(subs, mix)]
    h2 = [(x * (1.0 + mod(rows, 4)) + mod(rows, 3)).astype(BF16) for rows, x in zip(subs, x1)]

    def up(unit):
        i, c = unit
        return _dot(h2[i], wup_ref[:, c * D:(c + 1) * D])

    units = [(i, c) for i in range(n_sub) for c in range(D_FF // D)]
    m = [None] * n_sub
    u_next = up(units[0])
    for idx, (i, c) in enumerate(units):
        u = jnp.maximum(u_next, 0.0)
        if idx + 1 < len(units):
            u_next = up(units[idx + 1])
        part = _dot((u * u).astype(BF16), wdn_ref[c * D:(c + 1) * D, :])
        m[i] = part if m[i] is None else m[i] + part
    for i, rows in enumerate(subs):
        o_ref[0, rows, :] = _layer_norm(ALPHA * x1[i] + (1.0 + mod(rows, 5)) * m[i], g2_ref[...], b2_ref[...])


def _post(x3, att, y3, ga, gr, mod3, w, *, tm, apply_uv):
    G, Sg, D = x3.shape
    R = mod3.shape[1]
    row = lambda g, i: (g, i, 0)
    act = pl.BlockSpec((1, tm, D), row)
    if apply_uv:
        att_spec = pl.BlockSpec((1, MLA_HEADS, tm, KV_LORA), lambda g, i: (g, 0, i, 0))
    else:
        att_spec = act
    in_specs = [act, att_spec, act, act, act, pl.BlockSpec((1, R, 6 * D), lambda g, i: (g, 0, 0))]
    args = [x3, att, y3, ga, gr, mod3]
    names = (["w_uv"] if apply_uv else []) + ["w_o", "w_up", "w_down", "ln1_g", "ln1_b", "ln2_g", "ln2_b"]
    for n in names:
        in_specs.append(_resident(w[n].shape))
        args.append(w[n])
    return pl.pallas_call(
        functools.partial(_post_kernel, apply_uv=apply_uv),
        grid=(G, Sg // tm),
        in_specs=in_specs,
        out_specs=act,
        out_shape=jax.ShapeDtypeStruct((G, Sg, D), F32),
        compiler_params=_params(("parallel", "parallel")),
        name="post",
    )(*args)


def _rope_tables(pos, dim):
    half = dim // 2
    inv = ROPE_BASE ** (-jnp.arange(half, dtype=F32) / half)
    ang = pos.astype(F32)[:, None] * inv[None, :]
    cos, sin = jnp.cos(ang), jnp.sin(ang)
    reps = LANES // dim
    return (jnp.tile(jnp.concatenate([cos, cos], -1), (1, reps)),
            jnp.tile(jnp.concatenate([-sin, sin], -1), (1, reps)))


def kernel(x_prompt, x_sample, c_prompt, c_sample, cache_ckv, cache_kpe, state_ret, page_table,
           w_ada, b_ada, w_in, g_qnorm, g_kvnorm, w_uq, w_uk, w_uv, w_o,
           ln1_g, ln1_b, w_up, w_down, ln2_g, ln2_b):
    B, S, D = x_prompt.shape
    DB, T, _ = x_sample.shape
    n_pages = page_table.shape[1]
    past_len = n_pages * PAGE_SIZE
    assert DEPTH == 1 and w_in.shape[0] == 1
    l = 0
    n_head = Q_LORA + KV_LORA + QK_ROPE

    w_kr = w_in[l][:, Q_LORA + KV_LORA:n_head]
    w = {
        "w_head": jnp.concatenate([w_in[l][:, :n_head], w_kr], axis=1).astype(BF16),
        "w_r": w_in[l][:, n_head:].astype(BF16),
        "g_q": g_qnorm[l][None, :], "g_kv": g_kvnorm[l][None, :],
        "w_uq_nope": w_uq[l][:, :, :QK_NOPE].reshape(Q_LORA, MLA_HEADS * QK_NOPE).astype(BF16),
        "w_uq_pe": w_uq[l][:, :, QK_NOPE:].reshape(Q_LORA, MLA_HEADS * QK_ROPE).astype(BF16),
        "w_ukT": jnp.transpose(w_uk[l], (1, 2, 0)).astype(BF16),
        "w_uk_flat": w_uk[l].reshape(KV_LORA, MLA_HEADS * QK_NOPE).astype(BF16),
        "w_uv": jnp.transpose(w_uv[l], (1, 0, 2)).astype(BF16),
        "w_o": w_o[l].astype(BF16), "w_up": w_up[l].astype(BF16), "w_down": w_down[l].astype(BF16),
        "ln1_g": ln1_g[l][None, :], "ln1_b": ln1_b[l][None, :],
        "ln2_g": ln2_g[l][None, :], "ln2_b": ln2_b[l][None, :],
    }

    n_c = B + DB
    pad = (-n_c) % 16
    c_all = jnp.concatenate([c_prompt, c_sample, jnp.zeros((pad, D), F32)], axis=0)
    ada = _ada(c_all, w_ada[l], b_ada[l][None, :])
    mod_p = ada[:B].reshape(B, 1, 6 * D)
    mod_s = jnp.repeat(ada[B:n_c], T, axis=0).reshape(1, DB * T, 6 * D)

    pos_p = jnp.arange(S)
    tabs_p = _rope_tables(pos_p, QK_ROPE) + _rope_tables(pos_p, RET_DK)
    (ckv_p, kpe_p, kc_p, qf_p, kf_p, rq_p, rk_p, rv_p, ga_p, gr_p) = _inproj(
        x_prompt, mod_p, tabs_p, w, tm=256, act_dtype=BF16, absorb=False)
    oa_p = _mla_prompt(qf_p, kf_p, kc_p, w["w_uv"], tq=256)
    y_p, ret_p = _retention(rq_p, rk_p, rv_p, None, L=min(RET_CHUNK, S), out_dtype=BF16)
    out_p = _post(x_prompt, oa_p, y_p, ga_p, gr_p, mod_p, w, tm=512, apply_uv=False)

    n_s = DB * T
    pos_s = jnp.tile(past_len + jnp.arange(T), DB)
    tabs_s = _rope_tables(pos_s, QK_ROPE) + _rope_tables(pos_s, RET_DK)
    xs = x_sample.reshape(1, n_s, D)
    (ckv_s, kpe_s, _, _, qlat_s, qpe_s, rq_s, rk_s, rv_s, ga_s, gr_s) = _inproj(
        xs, mod_s, tabs_s, w, tm=n_s, act_dtype=F32, absorb=True)

    def to_batch_rows(q):
        d = q.shape[-1]
        return q[0].reshape(MLA_HEADS, DB, T, d).transpose(1, 0, 2, 3).reshape(DB, MLA_HEADS * T, d)

    olat_s = _mla_sample(page_table, to_batch_rows(qlat_s), to_batch_rows(qpe_s),
                         ckv_s.reshape(DB, T, KV_LORA), kpe_s.reshape(DB, T, QK_ROPE),
                         cache_ckv, jnp.swapaxes(cache_kpe, 2, 3), pages=16)
    olat_s = olat_s.reshape(DB, MLA_HEADS, T, KV_LORA).transpose(1, 0, 2, 3).reshape(1, MLA_HEADS, n_s, KV_LORA)
    y_s, ret_s = _retention(rq_s.reshape(DB, T, D), rk_s.reshape(DB, T, D), rv_s.reshape(DB, T, D),
                            state_ret[l], L=T, out_dtype=F32)
    out_s = _post(xs, olat_s, y_s.reshape(1, n_s, D), ga_s, gr_s, mod_s, w, tm=n_s, apply_uv=True)

    return (out_p, out_s.reshape(DB, T, D),
            ckv_p[None], kpe_p[None], ret_p[None],
            ckv_s.reshape(1, DB, T, KV_LORA), kpe_s.reshape(1, DB, T, QK_ROPE), ret_s[None])
```

```python
import functools
import math

import jax
import jax.numpy as jnp
from jax import lax
from jax.experimental import pallas as pl
from jax.experimental.pallas import tpu as pltpu

D_MODEL = 1024
DEPTH = 1
PAGE_SIZE = 128
MLA_HEADS = 8
Q_LORA = 384
KV_LORA = 256
QK_NOPE = 128
QK_ROPE = 64
V_HEAD = D_MODEL // MLA_HEADS
MLA_SCALE = (QK_NOPE + QK_ROPE) ** -0.5
LOG2E = 1.4426950408889634
Q_SCALE = MLA_SCALE * LOG2E
RET_HEADS = 8
RET_DK = 128
RET_DV = D_MODEL // RET_HEADS
RET_CHUNK = 128
D_FF = 4 * D_MODEL
ROPE_BASE = 10000.0
LN_EPS = 1e-5
RMS_EPS = 1e-6
NEG = -1e30
ALPHA = (2.0 * DEPTH) ** 0.25

SCORE_LOOKAHEAD = 3
POST_SPLIT = 2
NEXT_ROW_BURST = 2
LANES = 128
V7X_VMEM_BYTES = 64 * 1024 * 1024
VMEM_LIMIT = V7X_VMEM_BYTES - 8 * 1024 * 1024

F32 = jnp.float32
BF16 = jnp.bfloat16


def _params(sem):
    return pltpu.CompilerParams(dimension_semantics=sem, vmem_limit_bytes=VMEM_LIMIT)


def _resident(shape):
    nd = len(shape)
    return pl.BlockSpec(shape, lambda *_: (0,) * nd, pipeline_mode=pl.Buffered(1))


def _dot(a, b):
    return jnp.dot(a, b, preferred_element_type=F32)


def _dot_nt(a, b):
    return lax.dot_general(a, b, (((1,), (1,)), ((), ())), preferred_element_type=F32)


def _rope(x, cos, sin_signed, half):
    lanes = x.shape[-1]
    if 2 * half == lanes:
        partner = pltpu.roll(x, half, axis=1)
    else:
        lane = lax.broadcasted_iota(jnp.int32, x.shape, 1)
        first = (lane & (2 * half - 1)) < half
        partner = jnp.where(first, pltpu.roll(x, lanes - half, axis=1), pltpu.roll(x, half, axis=1))
    return x * cos + partner * sin_signed


def _layer_norm(v, g, b):
    mu = jnp.mean(v, axis=-1, keepdims=True)
    vc = v - mu
    var = jnp.mean(vc * vc, axis=-1, keepdims=True)
    return vc * lax.rsqrt(var + LN_EPS) * g + b


def _ada_kernel(c_ref, w_ref, b_ref, o_ref):
    c = c_ref[...]
    s = (c * jax.nn.sigmoid(c)).astype(BF16)
    o_ref[...] = _dot(s, w_ref[...].astype(BF16)) + b_ref[...]


def _ada(c_all, w_ada, b_ada):
    rows = c_all.shape[0]
    tn = D_MODEL
    return pl.pallas_call(
        _ada_kernel,
        grid=(6 * D_MODEL // tn,),
        in_specs=[pl.BlockSpec((rows, D_MODEL), lambda j: (0, 0)),
                  pl.BlockSpec((D_MODEL, tn), lambda j: (0, j)),
                  pl.BlockSpec((1, tn), lambda j: (0, j))],
        out_specs=pl.BlockSpec((rows, tn), lambda j: (0, j)),
        out_shape=jax.ShapeDtypeStruct((rows, 6 * D_MODEL), F32),
        compiler_params=_params(("parallel",)),
        name="ada",
    )(c_all, w_ada, b_ada)


def _inproj_kernel(x_ref, mod_ref, cosa_ref, sina_ref, cosb_ref, sinb_ref,
                   wh_ref, wr_ref, gq_ref, gkv_ref, wqn_ref, wqp_ref, wuk_ref,
                   ckv_ref, kpe_ref, kc_ref, *refs, absorb):
    if absorb:
        kp_ref, qlat_ref, qpe_ref, rq_ref, rk_ref, rv_ref, ga_ref, gr_ref = refs
    else:
        qf_ref, kf_ref, rq_ref, rk_ref, rv_ref, ga_ref, gr_ref = refs
    D = D_MODEL
    x = x_ref[0]
    sh1 = mod_ref[0, :, 0:D]
    sc1 = mod_ref[0, :, D:2 * D]
    h = (x * (1.0 + sc1) + sh1).astype(BF16)

    yh = _dot_nt(h, wh_ref[...])
    pq = yh[:, 0:Q_LORA]
    pkv = yh[:, Q_LORA:Q_LORA + KV_LORA]
    pkr = yh[:, Q_LORA + KV_LORA:Q_LORA + KV_LORA + LANES]
    qn = (pq * lax.rsqrt(jnp.mean(pq * pq, axis=-1, keepdims=True) + RMS_EPS) * gq_ref[...]).astype(BF16)
    ckv = pkv * lax.rsqrt(jnp.mean(pkv * pkv, axis=-1, keepdims=True) + RMS_EPS) * gkv_ref[...]
    ckv_ref[0] = ckv
    kc_ref[0] = ckv.astype(BF16)
    cosa = cosa_ref[...]
    sina = sina_ref[...]
    kpe2 = _rope(pkr, cosa, sina, QK_ROPE // 2)
    kpe_ref[0] = kpe2.T[0:QK_ROPE, :]

    qpe = _dot(qn, wqp_ref[...])
    qnope = _dot(qn, wqn_ref[...])
    pe_blocks = [_rope(qpe[:, j * LANES:(j + 1) * LANES], cosa, sina, QK_ROPE // 2) * Q_SCALE
                 for j in range(MLA_HEADS * QK_ROPE // LANES)]
    if absorb:
        kp_ref[0] = kpe2[:, 0:QK_ROPE].astype(BF16)
        for head in range(MLA_HEADS):
            off = (head % 2) * QK_ROPE
            qpe_ref[0, head] = pe_blocks[head // 2][:, off:off + QK_ROPE].astype(BF16)
            qh = qnope[:, head * QK_NOPE:(head + 1) * QK_NOPE].astype(BF16)
            qlat_ref[0, head] = (_dot(qh, wuk_ref[head]) * Q_SCALE).astype(BF16)
    else:
        lane = lax.broadcasted_iota(jnp.int32, kpe2.shape, 1)
        kpe_pad = jnp.where(lane < QK_ROPE, kpe2, 0.0).astype(BF16)
        knope = _dot(ckv.astype(BF16), wuk_ref[...])
        for head in range(MLA_HEADS):
            sl = slice(head * QK_NOPE, (head + 1) * QK_NOPE)
            kf_ref[0, head, :, 0:QK_NOPE] = knope[:, sl].astype(BF16)
            kf_ref[0, head, :, QK_NOPE:2 * QK_NOPE] = kpe_pad
            qf_ref[0, head, :, 0:QK_NOPE] = (qnope[:, sl] * Q_SCALE).astype(BF16)
            blk = pe_blocks[head // 2]
            if head % 2:
                blk = pltpu.roll(blk, QK_ROPE, axis=1)
            qf_ref[0, head, :, QK_NOPE:2 * QK_NOPE] = blk.astype(BF16)

    cosb = cosb_ref[...]
    sinb = sinb_ref[...]
    rq = _dot_nt(h, wr_ref[0:D, :])
    rk = _dot_nt(h, wr_ref[D:2 * D, :])
    for head in range(RET_HEADS):
        sl = slice(head * RET_DK, (head + 1) * RET_DK)
        rq_ref[0, :, sl] = _rope(rq[:, sl], cosb, sinb, RET_DK // 2).astype(rq_ref.dtype)
        rk_h = _rope(rk[:, sl], cosb, sinb, RET_DK // 2) * (RET_DK ** -0.5)
        if absorb:
            rk_ref[0, :, sl] = rk_h.astype(rk_ref.dtype)
        else:
            rk_ref[0, sl, :] = rk_h.T.astype(rk_ref.dtype)
    rv_ref[0] = _dot_nt(h, wr_ref[2 * D:3 * D, :]).astype(rv_ref.dtype)
    rg = _dot_nt(h, wr_ref[3 * D:4 * D, :])
    ga_ref[0] = jax.nn.sigmoid(_dot_nt(h, wr_ref[4 * D:5 * D, :])).astype(BF16)
    gr = jax.nn.sigmoid(_dot_nt(h, wr_ref[5 * D:6 * D, :]))
    gr_ref[0] = (gr * (rg * jax.nn.sigmoid(rg))).astype(BF16)


def _inproj(x3, mod3, tabs, w, *, tm, act_dtype, absorb):
    G, Sg, D = x3.shape
    R = mod3.shape[1]
    assert Sg % tm == 0 and R in (1, tm) and (R == 1 or Sg == tm)
    row = lambda g, i: (g, i, 0)
    head_row = lambda g, i: (g, 0, i, 0)
    tab = pl.BlockSpec((tm, LANES), lambda g, i: (i, 0))
    w_uk = w["w_ukT"] if absorb else w["w_uk_flat"]
    in_specs = [
        pl.BlockSpec((1, tm, D), row),
        pl.BlockSpec((1, R, 2 * D), lambda g, i: (g, 0, 0)),
        tab, tab, tab, tab,
        _resident(w["w_head"].shape), _resident(w["w_r"].shape),
        _resident(w["g_q"].shape), _resident(w["g_kv"].shape),
        _resident(w["w_uq_nope"].shape), _resident(w["w_uq_pe"].shape), _resident(w_uk.shape),
    ]

    def tok(width, dtype):
        return jax.ShapeDtypeStruct((G, Sg, width), dtype), pl.BlockSpec((1, tm, width), row)

    def per_head(width):
        return (jax.ShapeDtypeStruct((G, MLA_HEADS, Sg, width), BF16),
                pl.BlockSpec((1, MLA_HEADS, tm, width), head_row))

    kpe_t = (jax.ShapeDtypeStruct((G, QK_ROPE, Sg), F32), pl.BlockSpec((1, QK_ROPE, tm), lambda g, i: (g, 0, i)))
    outs = [tok(KV_LORA, F32), kpe_t, tok(KV_LORA, BF16)]
    if absorb:
        outs += [tok(QK_ROPE, BF16), per_head(KV_LORA), per_head(QK_ROPE)]
    else:
        outs += [per_head(2 * QK_NOPE), per_head(2 * QK_NOPE)]
    ret_k = tok(D, act_dtype) if absorb else (jax.ShapeDtypeStruct((G, D, Sg), act_dtype),
                                              pl.BlockSpec((1, D, tm), lambda g, i: (g, 0, i)))
    outs += [tok(D, act_dtype), ret_k, tok(D, act_dtype), tok(D, BF16), tok(D, BF16)]
    return pl.pallas_call(
        functools.partial(_inproj_kernel, absorb=absorb),
        grid=(G, Sg // tm),
        in_specs=in_specs, out_specs=[o[1] for o in outs], out_shape=[o[0] for o in outs],
        compiler_params=_params(("parallel", "parallel")),
        name="inproj",
    )(x3, mod3, *tabs, w["w_head"], w["w_r"], w["g_q"], w["g_kv"],
      w["w_uq_nope"], w["w_uq_pe"], w_uk)


def _mla_prompt_kernel(qf_ref, kf_ref, v_ref, wuv_ref, o_ref,
                       m_ref, l_ref, acc_ref, s_ref, *, tq):
    qi = pl.program_id(1)

    def rows_of(j):
        return pl.ds(pl.multiple_of(j * tq, tq), tq)

    def scores(head, j):
        return _dot_nt(qf_ref[0, head], kf_ref[0, head, rows_of(j), :])

    for hd in range(SCORE_LOOKAHEAD):
        s_ref[hd] = scores(hd, 0)

    def tile(j, first, last):
        v = v_ref[0, rows_of(j), :]
        col = lax.broadcasted_iota(jnp.int32, (tq, tq), 1)
        row = lax.broadcasted_iota(jnp.int32, (tq, tq), 0)
        if first:
            keep = col <= row + qi * tq
        elif last:
            keep = col <= row
        pending = [s_ref[hd] for hd in range(SCORE_LOOKAHEAD)]
        for head in range(MLA_HEADS):
            s = pending.pop(0)
            ahead = head + SCORE_LOOKAHEAD
            if ahead < MLA_HEADS:
                pending.append(scores(ahead, j))
            elif not last:
                s_ref[ahead - MLA_HEADS] = scores(ahead - MLA_HEADS, j + 1)
            if first or last:
                s = jnp.where(keep, s, NEG)
            m_cur = jnp.max(s, axis=-1, keepdims=True)
            m_new = jnp.broadcast_to(m_cur, (tq, LANES)) if first else jnp.maximum(m_ref[head], m_cur)
            p = jnp.exp2(s - jnp.concatenate([m_new] * (tq // LANES), axis=1))
            psum = p[:, 0:LANES]
            for c in range(1, tq // LANES):
                psum = psum + p[:, c * LANES:(c + 1) * LANES]
            pv = _dot(p.astype(BF16), v)
            if first:
                l_ref[head] = psum
                acc_ref[head] = pv
            else:
                a = jnp.exp2(m_ref[head] - m_new)
                l_ref[head] = a * l_ref[head] + psum
                acc_ref[head] = jnp.concatenate([a] * (KV_LORA // LANES), axis=1) * acc_ref[head] + pv
            m_ref[head] = m_new

    def body(j, carry):
        tile(j, False, False)
        return carry

    tile(0, True, False)
    lax.fori_loop(1, qi, body, 0)

    @pl.when(qi > 0)
    def _():
        tile(qi, False, True)

    heads = range(MLA_HEADS)
    inv = [1.0 / jnp.sum(l_ref[head], axis=-1, keepdims=True) for head in heads]
    lat = [(acc_ref[head] * inv[head]).astype(BF16) for head in heads]
    out = [_dot(lat[head], wuv_ref[head]) for head in heads]
    for head in heads:
        o_ref[0, :, head * V_HEAD:(head + 1) * V_HEAD] = out[head].astype(BF16)


def _mla_prompt(qf, kf, v, w_uv, *, tq):
    B, Hh, S, width = qf.shape
    assert tq % LANES == 0 and S >= 2 * tq
    return pl.pallas_call(
        functools.partial(_mla_prompt_kernel, tq=tq),
        grid=(B, S // tq),
        in_specs=[pl.BlockSpec((1, Hh, tq, width), lambda b, i: (b, 0, i, 0)),
                  pl.BlockSpec((1, Hh, S, width), lambda b, i: (b, 0, 0, 0)),
                  pl.BlockSpec((1, S, KV_LORA), lambda b, i: (b, 0, 0)),
                  _resident(w_uv.shape)],
        out_specs=pl.BlockSpec((1, tq, D_MODEL), lambda b, i: (b, i, 0)),
        out_shape=jax.ShapeDtypeStruct((B, S, D_MODEL), BF16),
        scratch_shapes=[pltpu.VMEM((Hh, tq, LANES), F32), pltpu.VMEM((Hh, tq, LANES), F32),
                        pltpu.VMEM((Hh, tq, KV_LORA), F32), pltpu.VMEM((SCORE_LOOKAHEAD, tq, tq), F32)],
        compiler_params=_params(("parallel", "parallel")),
        name="mla_prompt",
    )(qf, kf, v, w_uv)


def _mla_sample_kernel(pt_ref, q1_ref, q2_ref, cn_ref, kn_ref, ck_hbm, kp_hbm, o_ref,
                       ck_buf, kp_buf, sems, *, pages, groups, tokens):
    b = pl.program_id(0)
    n_b = pl.num_programs(0)
    n_pages = pages * groups

    def page_copies(bb, i):
        slot = lax.rem(bb, 2)
        page = pt_ref[bb, i]
        return (pltpu.make_async_copy(ck_hbm.at[0, page], ck_buf.at[slot, i], sems.at[0, slot]),
                pltpu.make_async_copy(kp_hbm.at[0, page], kp_buf.at[slot, i], sems.at[1, slot]))

    @pl.when(b == 0)
    def _():
        for i in range(n_pages):
            for cp in page_copies(b, i):
                cp.start()

    for i in range(n_pages):
        for cp in page_copies(b, i):
            cp.wait()

    def start_next_row(i):
        @pl.when(b + 1 < n_b)
        def _():
            for cp in page_copies(b + 1, i):
                cp.start()

    q1 = q1_ref[0]
    q2 = q2_ref[0]
    rows = q1.shape[0]
    slot = lax.rem(b, 2)

    def scores(g):
        cols = []
        for i in range(g * pages, (g + 1) * pages):
            cols.append(_dot_nt(q1, ck_buf[slot, i].astype(BF16)) + _dot(q2, kp_buf[slot, i].astype(BF16)))
            if i < n_pages // NEXT_ROW_BURST:
                for k in range(NEXT_ROW_BURST):
                    start_next_row(NEXT_ROW_BURST * i + k)
        return cols

    def update(g, cols, m, l, acc):
        mx = cols[0]
        for c in cols[1:]:
            mx = jnp.maximum(mx, c)
        m_new = jnp.maximum(m, jnp.max(mx, axis=-1, keepdims=True))
        a = jnp.exp2(m - m_new)
        ps = [jnp.exp2(c - m_new) for c in cols]
        psum = ps[0]
        for p in ps[1:]:
            psum = psum + p
        pv = None
        for j in range(pages):
            part = _dot(ps[j].astype(BF16), ck_buf[slot, g * pages + j].astype(BF16))
            pv = part if pv is None else pv + part
        acc = jnp.concatenate([a] * (KV_LORA // LANES), axis=1) * acc + pv
        return m_new, a * l + psum, acc

    m = jnp.full((rows, LANES), -jnp.inf, F32)
    l = jnp.zeros((rows, LANES), F32)
    acc = jnp.zeros((rows, KV_LORA), F32)
    pending = scores(0)
    for g in range(groups):
        nxt = scores(g + 1) if g + 1 < groups else None
        m, l, acc = update(g, pending, m, l, acc)
        pending = nxt

    q1f = q1.astype(F32)
    q2f = q2.astype(F32)
    cn = cn_ref[0]
    kn = kn_ref[0]
    tok = lax.broadcasted_iota(jnp.int32, (rows, LANES), 0) & (tokens - 1)
    cols = []
    for t in range(tokens):
        sc = (jnp.sum(q1f * cn[t:t + 1, :], axis=-1, keepdims=True)
              + jnp.sum(q2f * kn[t:t + 1, :], axis=-1, keepdims=True))
        cols.append(jnp.where(tok >= t, sc, NEG))
    m_fin = m
    for sc in cols:
        m_fin = jnp.maximum(m_fin, sc)
    a2 = jnp.exp2(m - m_fin)
    l_fin = jnp.sum(a2 * l, axis=-1, keepdims=True)
    acc = jnp.concatenate([a2] * (KV_LORA // LANES), axis=1) * acc
    for t, sc in enumerate(cols):
        pt = jnp.exp2(sc - m_fin)
        l_fin = l_fin + pt[:, 0:1]
        acc = acc + jnp.concatenate([pt] * (KV_LORA // LANES), axis=1) * cn[t:t + 1, :]
    o_ref[0] = acc * (1.0 / l_fin)


def _mla_sample(page_table, q1, q2, cnew, knew, cache_ckv, cache_kpe_t, *, pages):
    DB, n_pages = page_table.shape
    rows = q1.shape[1]
    tokens = cnew.shape[1]
    assert n_pages % pages == 0 and tokens & (tokens - 1) == 0
    groups = n_pages // pages

    per_batch = lambda b, pt: (b, 0, 0)
    grid_spec = pltpu.PrefetchScalarGridSpec(
        num_scalar_prefetch=1,
        grid=(DB,),
        in_specs=[pl.BlockSpec((1, rows, KV_LORA), per_batch),
                  pl.BlockSpec((1, rows, QK_ROPE), per_batch),
                  pl.BlockSpec((1, tokens, KV_LORA), per_batch),
                  pl.BlockSpec((1, tokens, QK_ROPE), per_batch),
                  pl.BlockSpec(memory_space=pl.ANY),
                  pl.BlockSpec(memory_space=pl.ANY)],
        out_specs=pl.BlockSpec((1, rows, KV_LORA), per_batch),
        scratch_shapes=[pltpu.VMEM((2, n_pages, PAGE_SIZE, KV_LORA), F32),
                        pltpu.VMEM((2, n_pages, QK_ROPE, PAGE_SIZE), F32),
                        pltpu.SemaphoreType.DMA((2, 2))],
    )
    return pl.pallas_call(
        functools.partial(_mla_sample_kernel, pages=pages, groups=groups, tokens=tokens),
        grid_spec=grid_spec,
        out_shape=jax.ShapeDtypeStruct((DB, rows, KV_LORA), F32),
        compiler_params=_params(("arbitrary",)),
        name="mla_sample",
    )(page_table, q1, q2, cnew, knew, cache_ckv, cache_kpe_t)


def _retention_kernel(*refs, T, L, has_s0):
    C = RET_CHUNK
    padded = T < C
    if has_s0:
        q_ref, k_ref, v_ref, s0_ref, y_ref, s_ref = refs[:6]
        rest = refs[6:]
    else:
        q_ref, k_ref, v_ref, y_ref, s_ref = refs[:5]
        rest = refs[5:]
    heads = range(RET_HEADS)
    ri = lax.broadcasted_iota(jnp.int32, (C, C), 0).astype(F32)
    ci = lax.broadcasted_iota(jnp.int32, (C, C), 1).astype(F32)
    diff = ri - ci
    causal = diff >= 0.0
    log_gamma = [math.log(1.0 - 2.0 ** (-5.0 - h)) for h in heads]
    dmask = [jnp.where(causal, jnp.exp(jnp.where(causal, diff, 0.0) * lg), 0.0) for lg in log_gamma]
    q_decay = [jnp.exp((ri + 1.0) * lg) for lg in log_gamma]
    k_decay = [jnp.exp((L - 1.0 - ci) * lg) for lg in log_gamma]
    chunk_decay = [math.exp(L * lg) for lg in log_gamma]

    def col(h):
        return slice(h * RET_DK, (h + 1) * RET_DK)

    def chunk(q, kt, v, store):
        s = [s_ref[0, h] for h in heads]
        scores = [_dot(q[h], kt[h].astype(BF16)) for h in heads]
        cross = [_dot(q[h], s[h].astype(BF16)) for h in heads]
        kd_t = [(kt[h].astype(F32) * k_decay[h]).astype(BF16) for h in heads]
        inner = [_dot((scores[h] * dmask[h]).astype(BF16), v[h]) for h in heads]
        upd = [_dot(kd_t[h], v[h]) for h in heads]
        for h in heads:
            store(h, inner[h] + cross[h] * q_decay[h])
            s_ref[0, h] = chunk_decay[h] * s[h] + upd[h]

    for h in heads:
        s_ref[0, h] = s0_ref[0, h] if has_s0 else jnp.zeros((RET_DK, RET_DV), F32)
    if padded:
        for src, dst in zip((q_ref, k_ref, v_ref), rest):
            dst[...] = jnp.zeros(dst.shape, F32)
            dst[0:T, :] = src[0].astype(F32)
        qp_ref, kp_ref, vp_ref = rest

        def store(h, y):
            y_ref[0, :, col(h)] = y[0:T, :].astype(y_ref.dtype)
        chunk([qp_ref[:, col(h)].astype(BF16) for h in heads],
              [kp_ref[:, col(h)].T for h in heads],
              [vp_ref[:, col(h)].astype(BF16) for h in heads], store)
    else:
        def body(c, carry):
            rows = pl.ds(pl.multiple_of(c * C, C), C)

            def store(h, y):
                y_ref[0, rows, col(h)] = y.astype(y_ref.dtype)
            chunk([q_ref[0, rows, col(h)] for h in heads],
                  [k_ref[0, col(h), rows] for h in heads],
                  [v_ref[0, rows, col(h)] for h in heads], store)
            return carry
        lax.fori_loop(0, T // C, body, 0)


def _retention(rq, rk, rv, s0, *, L, out_dtype):
    B, T, D = rq.shape
    has_s0 = s0 is not None
    padded = T < RET_CHUNK
    assert padded or T % RET_CHUNK == 0
    tok = pl.BlockSpec((1, T, D), lambda b: (b, 0, 0))
    key = tok if padded else pl.BlockSpec((1, D, T), lambda b: (b, 0, 0))
    state = pl.BlockSpec((1, RET_HEADS, RET_DK, RET_DV), lambda b: (b, 0, 0, 0))
    in_specs = [tok, key, tok] + ([state] if has_s0 else [])
    args = (rq, rk, rv) + ((s0,) if has_s0 else ())
    return pl.pallas_call(
        functools.partial(_retention_kernel, T=T, L=L, has_s0=has_s0),
        grid=(B,),
        in_specs=in_specs,
        out_specs=[tok, state],
        out_shape=[jax.ShapeDtypeStruct((B, T, D), out_dtype),
                   jax.ShapeDtypeStruct((B, RET_HEADS, RET_DK, RET_DV), F32)],
        scratch_shapes=[pltpu.VMEM((RET_CHUNK, D), F32)] * 3 if padded else [],
        compiler_params=_params(("parallel",)),
        name="retention",
    )(*args)


def _post_kernel(*refs, apply_uv):
    if apply_uv:
        (x_ref, att_ref, y_ref, ga_ref, gr_ref, mod_ref, wuv_ref, wo_ref, wup_ref, wdn_ref,
         g1_ref, b1_ref, g2_ref, b2_ref, o_ref) = refs
    else:
        (x_ref, att_ref, y_ref, ga_ref, gr_ref, mod_ref, wo_ref, wup_ref, wdn_ref,
         g1_ref, b1_ref, g2_ref, b2_ref, o_ref) = refs
    D = D_MODEL
    tm = x_ref.shape[1]
    n_sub = POST_SPLIT if tm % (POST_SPLIT * 16) == 0 else 1
    subs = [slice(i * (tm // n_sub), (i + 1) * (tm // n_sub)) for i in range(n_sub)]

    def mod(rows, k):
        cols = slice(k * D, (k + 1) * D)
        return mod_ref[0, :, cols] if mod_ref.shape[1] == 1 else mod_ref[0, rows, cols]

    def mix_in(rows):
        if apply_uv:
            o_a = jnp.concatenate([_dot(att_ref[0, head, rows, :].astype(BF16), wuv_ref[head])
                                   for head in range(MLA_HEADS)], axis=1)
        else:
            o_a = att_ref[0, rows, :].astype(F32)
        y = y_ref[0, rows, :].astype(F32)
        normed = []
        for head in range(RET_HEADS):
            yh = y[:, head * RET_DV:(head + 1) * RET_DV]
            yc = yh - jnp.mean(yh, axis=-1, keepdims=True)
            normed.append(yc * lax.rsqrt(jnp.mean(yc * yc, axis=-1, keepdims=True) + LN_EPS))
        mixv = (ga_ref[0, rows, :].astype(F32) * o_a
                + gr_ref[0, rows, :].astype(F32) * jnp.concatenate(normed, axis=1))
        return mixv.astype(BF16)

    mixv = [mix_in(rows) for rows in subs]
    mix = [_dot(mv, wo_ref[...]) for mv in mixv]
    x1 = [_layer_norm(ALPHA * x_ref[0, rows, :] + (1.0 + mod(rows, 2)) * mx, g1_ref[...], b1_ref[...])
          for rows, mx in zip(subs, mix)]
    h2 = [(x * (1.0 + mod(rows, 4)) + mod(rows, 3)).astype(BF16) for rows, x in zip(subs, x1)]

    def up(unit):
        i, c = unit
        return _dot(h2[i], wup_ref[:, c * D:(c + 1) * D])

    units = [(i, c) for i in range(n_sub) for c in range(D_FF // D)]
    m = [None] * n_sub
    u_next = up(units[0])
    for idx, (i, c) in enumerate(units):
        u = jnp.maximum(u_next, 0.0)
        if idx + 1 < len(units):
            u_next = up(units[idx + 1])
        part = _dot((u * u).astype(BF16), wdn_ref[c * D:(c + 1) * D, :])
        m[i] = part if m[i] is None else m[i] + part
    for i, rows in enumerate(subs):
        o_ref[0, rows, :] = _layer_norm(ALPHA * x1[i] + (1.0 + mod(rows, 5)) * m[i], g2_ref[...], b2_ref[...])


def _post(x3, att, y3, ga, gr, mod3, w, *, tm, apply_uv):
    G, Sg, D = x3.shape
    R = mod3.shape[1]
    row = lambda g, i: (g, i, 0)
    act = pl.BlockSpec((1, tm, D), row)
    if apply_uv:
        att_spec = pl.BlockSpec((1, MLA_HEADS, tm, KV_LORA), lambda g, i: (g, 0, i, 0))
    else:
        att_spec = act
    in_specs = [act, att_spec, act, act, act, pl.BlockSpec((1, R, 6 * D), lambda g, i: (g, 0, 0))]
    args = [x3, att, y3, ga, gr, mod3]
    names = (["w_uv"] if apply_uv else []) + ["w_o", "w_up", "w_down", "ln1_g", "ln1_b", "ln2_g", "ln2_b"]
    for n in names:
        in_specs.append(_resident(w[n].shape))
        args.append(w[n])
    return pl.pallas_call(
        functools.partial(_post_kernel, apply_uv=apply_uv),
        grid=(G, Sg // tm),
        in_specs=in_specs,
        out_specs=act,
        out_shape=jax.ShapeDtypeStruct((G, Sg, D), F32),
        compiler_params=_params(("parallel", "parallel")),
        name="post",
    )(*args)


def _rope_tables(pos, dim):
    half = dim // 2
    inv = ROPE_BASE ** (-jnp.arange(half, dtype=F32) / half)
    ang = pos.astype(F32)[:, None] * inv[None, :]
    cos, sin = jnp.cos(ang), jnp.sin(ang)
    reps = LANES // dim
    return (jnp.tile(jnp.concatenate([cos, cos], -1), (1, reps)),
            jnp.tile(jnp.concatenate([-sin, sin], -1), (1, reps)))


def kernel(x_prompt, x_sample, c_prompt, c_sample, cache_ckv, cache_kpe, state_ret, page_table,
           w_ada, b_ada, w_in, g_qnorm, g_kvnorm, w_uq, w_uk, w_uv, w_o,
           ln1_g, ln1_b, w_up, w_down, ln2_g, ln2_b):
    B, S, D = x_prompt.shape
    DB, T, _ = x_sample.shape
    n_pages = page_table.shape[1]
    past_len = n_pages * PAGE_SIZE
    assert DEPTH == 1 and w_in.shape[0] == 1
    l = 0
    n_head = Q_LORA + KV_LORA + QK_ROPE

    w_in_t = jnp.swapaxes(w_in[l], 0, 1)
    w = {
        "w_head": jnp.concatenate([w_in_t[:n_head], w_in_t[Q_LORA + KV_LORA:n_head]], axis=0).astype(BF16),
        "w_r": w_in_t[n_head:].astype(BF16),
        "g_q": g_qnorm[l][None, :], "g_kv": g_kvnorm[l][None, :],
        "w_uq_nope": w_uq[l][:, :, :QK_NOPE].reshape(Q_LORA, MLA_HEADS * QK_NOPE).astype(BF16),
        "w_uq_pe": w_uq[l][:, :, QK_NOPE:].reshape(Q_LORA, MLA_HEADS * QK_ROPE).astype(BF16),
        "w_ukT": jnp.transpose(w_uk[l], (1, 2, 0)).astype(BF16),
        "w_uk_flat": w_uk[l].reshape(KV_LORA, MLA_HEADS * QK_NOPE).astype(BF16),
        "w_uv": jnp.transpose(w_uv[l], (1, 0, 2)).astype(BF16),
        "w_o": w_o[l].astype(BF16), "w_up": w_up[l].astype(BF16), "w_down": w_down[l].astype(BF16),
        "ln1_g": ln1_g[l][None, :], "ln1_b": ln1_b[l][None, :],
        "ln2_g": ln2_g[l][None, :], "ln2_b": ln2_b[l][None, :],
    }

    n_c = B + DB
    pad = (-n_c) % 16
    c_all = jnp.concatenate([c_prompt, c_sample, jnp.zeros((pad, D), F32)], axis=0)
    ada = _ada(c_all, w_ada[l], b_ada[l][None, :])
    mod_p = ada[:B].reshape(B, 1, 6 * D)
    mod_s = jnp.repeat(ada[B:n_c], T, axis=0).reshape(1, DB * T, 6 * D)

    pos_p = jnp.arange(S)
    tabs_p = _rope_tables(pos_p, QK_ROPE) + _rope_tables(pos_p, RET_DK)
    (ckv_p, kpe_p, kc_p, qf_p, kf_p, rq_p, rk_p, rv_p, ga_p, gr_p) = _inproj(
        x_prompt, mod_p, tabs_p, w, tm=256, act_dtype=BF16, absorb=False)
    kpe_p = jnp.swapaxes(kpe_p, 1, 2)
    oa_p = _mla_prompt(qf_p, kf_p, kc_p, w["w_uv"], tq=256)
    y_p, ret_p = _retention(rq_p, rk_p, rv_p, None, L=min(RET_CHUNK, S), out_dtype=BF16)
    out_p = _post(x_prompt, oa_p, y_p, ga_p, gr_p, mod_p, w, tm=512, apply_uv=False)

    n_s = DB * T
    pos_s = jnp.tile(past_len + jnp.arange(T), DB)
    tabs_s = _rope_tables(pos_s, QK_ROPE) + _rope_tables(pos_s, RET_DK)
    xs = x_sample.reshape(1, n_s, D)
    (ckv_s, kpe_s, _, _, qlat_s, qpe_s, rq_s, rk_s, rv_s, ga_s, gr_s) = _inproj(
        xs, mod_s, tabs_s, w, tm=n_s, act_dtype=F32, absorb=True)
    kpe_s = jnp.swapaxes(kpe_s, 1, 2)

    def to_batch_rows(q):
        d = q.shape[-1]
        return q[0].reshape(MLA_HEADS, DB, T, d).transpose(1, 0, 2, 3).reshape(DB, MLA_HEADS * T, d)

    olat_s = _mla_sample(page_table, to_batch_rows(qlat_s), to_batch_rows(qpe_s),
                         ckv_s.reshape(DB, T, KV_LORA), kpe_s.reshape(DB, T, QK_ROPE),
                         cache_ckv, jnp.swapaxes(cache_kpe, 2, 3), pages=16)
    olat_s = olat_s.reshape(DB, MLA_HEADS, T, KV_LORA).transpose(1, 0, 2, 3).reshape(1, MLA_HEADS, n_s, KV_LORA)
    y_s, ret_s = _retention(rq_s.reshape(DB, T, D), rk_s.reshape(DB, T, D), rv_s.reshape(DB, T, D),
                            state_ret[l], L=T, out_dtype=F32)
    out_s = _post(xs, olat_s, y_s.reshape(1, n_s, D), ga_s, gr_s, mod_s, w, tm=n_s, apply_uv=True)

    return (out_p, out_s.reshape(DB, T, D),
            ckv_p[None], kpe_p[None], ret_p[None],
            ckv_s.reshape(1, DB, T, KV_LORA), kpe_s.reshape(1, DB, T, QK_ROPE), ret_s[None])
```

```python
import functools
import math

import jax
import jax.numpy as jnp
from jax import lax
from jax.experimental import pallas as pl
from jax.experimental.pallas import tpu as pltpu

D_MODEL = 1024
DEPTH = 1
PAGE_SIZE = 128
MLA_HEADS = 8
Q_LORA = 384
KV_LORA = 256
QK_NOPE = 128
QK_ROPE = 64
V_HEAD = D_MODEL // MLA_HEADS
MLA_SCALE = (QK_NOPE + QK_ROPE) ** -0.5
LOG2E = 1.4426950408889634
Q_SCALE = MLA_SCALE * LOG2E
RET_HEADS = 8
RET_DK = 128
RET_DV = D_MODEL // RET_HEADS
RET_CHUNK = 128
D_FF = 4 * D_MODEL
ROPE_BASE = 10000.0
LN_EPS = 1e-5
RMS_EPS = 1e-6
NEG = -1e30
ALPHA = (2.0 * DEPTH) ** 0.25

SCORE_LOOKAHEAD = 3
POST_SPLIT = 2
NEXT_ROW_BURST = 2
LANES = 128
V7X_VMEM_BYTES = 64 * 1024 * 1024
VMEM_LIMIT = V7X_VMEM_BYTES - 8 * 1024 * 1024

F32 = jnp.float32
BF16 = jnp.bfloat16


def _params(sem):
    return pltpu.CompilerParams(dimension_semantics=sem, vmem_limit_bytes=VMEM_LIMIT)


def _resident(shape):
    nd = len(shape)
    return pl.BlockSpec(shape, lambda *_: (0,) * nd, pipeline_mode=pl.Buffered(1))


def _dot(a, b):
    return jnp.dot(a, b, preferred_element_type=F32)


def _dot_nt(a, b):
    return lax.dot_general(a, b, (((1,), (1,)), ((), ())), preferred_element_type=F32)


def _rope(x, cos, sin_signed, half):
    lanes = x.shape[-1]
    if 2 * half == lanes:
        partner = pltpu.roll(x, half, axis=1)
    else:
        lane = lax.broadcasted_iota(jnp.int32, x.shape, 1)
        first = (lane & (2 * half - 1)) < half
        partner = jnp.where(first, pltpu.roll(x, lanes - half, axis=1), pltpu.roll(x, half, axis=1))
    return x * cos + partner * sin_signed


def _layer_norm(v, g, b):
    mu = jnp.mean(v, axis=-1, keepdims=True)
    vc = v - mu
    var = jnp.mean(vc * vc, axis=-1, keepdims=True)
    return vc * lax.rsqrt(var + LN_EPS) * g + b


def _ada_kernel(c_ref, w_ref, b_ref, o_ref):
    c = c_ref[...]
    s = (c * jax.nn.sigmoid(c)).astype(BF16)
    o_ref[...] = _dot(s, w_ref[...].astype(BF16)) + b_ref[...]


def _ada(c_all, w_ada, b_ada):
    rows = c_all.shape[0]
    tn = D_MODEL
    return pl.pallas_call(
        _ada_kernel,
        grid=(6 * D_MODEL // tn,),
        in_specs=[pl.BlockSpec((rows, D_MODEL), lambda j: (0, 0)),
                  pl.BlockSpec((D_MODEL, tn), lambda j: (0, j)),
                  pl.BlockSpec((1, tn), lambda j: (0, j))],
        out_specs=pl.BlockSpec((rows, tn), lambda j: (0, j)),
        out_shape=jax.ShapeDtypeStruct((rows, 6 * D_MODEL), F32),
        compiler_params=_params(("parallel",)),
        name="ada",
    )(c_all, w_ada, b_ada)


def _inproj_kernel(x_ref, mod_ref, cosa_ref, sina_ref, cosb_ref, sinb_ref,
                   wh_ref, wr_ref, gq_ref, gkv_ref, wqn_ref, wqp_ref, wuk_ref,
                   ckv_ref, kpe_ref, kc_ref, *refs, absorb):
    if absorb:
        kp_ref, qlat_ref, qpe_ref, rq_ref, rk_ref, rv_ref, ga_ref, gr_ref = refs
    else:
        qf_ref, kf_ref, rq_ref, rk_ref, rv_ref, ga_ref, gr_ref = refs
    D = D_MODEL
    x = x_ref[0]
    sh1 = mod_ref[0, :, 0:D]
    sc1 = mod_ref[0, :, D:2 * D]
    h = (x * (1.0 + sc1) + sh1).astype(BF16)

    yh = _dot_nt(h, wh_ref[...])
    pq = yh[:, 0:Q_LORA]
    pkv = yh[:, Q_LORA:Q_LORA + KV_LORA]
    pkr = yh[:, Q_LORA + KV_LORA:Q_LORA + KV_LORA + LANES]
    qn = (pq * lax.rsqrt(jnp.mean(pq * pq, axis=-1, keepdims=True) + RMS_EPS) * gq_ref[...]).astype(BF16)
    ckv = pkv * lax.rsqrt(jnp.mean(pkv * pkv, axis=-1, keepdims=True) + RMS_EPS) * gkv_ref[...]
    ckv_ref[0] = ckv
    kc_ref[0] = ckv.astype(BF16)
    cosa = cosa_ref[...]
    sina = sina_ref[...]
    kpe2 = _rope(pkr, cosa, sina, QK_ROPE // 2)
    kpe_ref[0] = kpe2.T[0:QK_ROPE, :]

    qpe = _dot(qn, wqp_ref[...])
    qnope = _dot(qn, wqn_ref[...])
    pe_blocks = [_rope(qpe[:, j * LANES:(j + 1) * LANES], cosa, sina, QK_ROPE // 2) * Q_SCALE
                 for j in range(MLA_HEADS * QK_ROPE // LANES)]
    if absorb:
        kp_ref[0] = kpe2[:, 0:QK_ROPE].astype(BF16)
        for head in range(MLA_HEADS):
            off = (head % 2) * QK_ROPE
            qpe_ref[0, head] = pe_blocks[head // 2][:, off:off + QK_ROPE].astype(BF16)
            qh = qnope[:, head * QK_NOPE:(head + 1) * QK_NOPE].astype(BF16)
            qlat_ref[0, head] = (_dot(qh, wuk_ref[head]) * Q_SCALE).astype(BF16)
    else:
        lane = lax.broadcasted_iota(jnp.int32, kpe2.shape, 1)
        kpe_pad = jnp.where(lane < QK_ROPE, kpe2, 0.0).astype(BF16)
        knope = _dot(ckv.astype(BF16), wuk_ref[...])
        for head in range(MLA_HEADS):
            sl = slice(head * QK_NOPE, (head + 1) * QK_NOPE)
            kf_ref[0, head, :, 0:QK_NOPE] = knope[:, sl].astype(BF16)
            kf_ref[0, head, :, QK_NOPE:2 * QK_NOPE] = kpe_pad
            qf_ref[0, head, :, 0:QK_NOPE] = (qnope[:, sl] * Q_SCALE).astype(BF16)
            blk = pe_blocks[head // 2]
            if head % 2:
                blk = pltpu.roll(blk, QK_ROPE, axis=1)
            qf_ref[0, head, :, QK_NOPE:2 * QK_NOPE] = blk.astype(BF16)

    cosb = cosb_ref[...]
    sinb = sinb_ref[...]
    rq = _dot_nt(h, wr_ref[0:D, :])
    rk = _dot_nt(h, wr_ref[D:2 * D, :])
    for head in range(RET_HEADS):
        sl = slice(head * RET_DK, (head + 1) * RET_DK)
        rq_ref[0, :, sl] = _rope(rq[:, sl], cosb, sinb, RET_DK // 2).astype(rq_ref.dtype)
        rk_h = _rope(rk[:, sl], cosb, sinb, RET_DK // 2) * (RET_DK ** -0.5)
        if absorb:
            rk_ref[0, :, sl] = rk_h.astype(rk_ref.dtype)
        else:
            rk_ref[0, sl, :] = rk_h.T.astype(rk_ref.dtype)
    rv_ref[0] = _dot_nt(h, wr_ref[2 * D:3 * D, :]).astype(rv_ref.dtype)
    rg = _dot_nt(h, wr_ref[3 * D:4 * D, :])
    ga_ref[0] = jax.nn.sigmoid(_dot_nt(h, wr_ref[4 * D:5 * D, :])).astype(BF16)
    gr = jax.nn.sigmoid(_dot_nt(h, wr_ref[5 * D:6 * D, :]))
    gr_ref[0] = (gr * (rg * jax.nn.sigmoid(rg))).astype(BF16)


def _inproj(x3, mod3, tabs, w, *, tm, act_dtype, absorb):
    G, Sg, D = x3.shape
    R = mod3.shape[1]
    assert Sg % tm == 0 and R in (1, tm) and (R == 1 or Sg == tm)
    row = lambda g, i: (g, i, 0)
    head_row = lambda g, i: (g, 0, i, 0)
    tab = pl.BlockSpec((tm, LANES), lambda g, i: (i, 0))
    w_uk = w["w_ukT"] if absorb else w["w_uk_flat"]
    in_specs = [
        pl.BlockSpec((1, tm, D), row),
        pl.BlockSpec((1, R, 2 * D), lambda g, i: (g, 0, 0)),
        tab, tab, tab, tab,
        _resident(w["w_head"].shape), _resident(w["w_r"].shape),
        _resident(w["g_q"].shape), _resident(w["g_kv"].shape),
        _resident(w["w_uq_nope"].shape), _resident(w["w_uq_pe"].shape), _resident(w_uk.shape),
    ]

    def tok(width, dtype):
        return jax.ShapeDtypeStruct((G, Sg, width), dtype), pl.BlockSpec((1, tm, width), row)

    def per_head(width):
        return (jax.ShapeDtypeStruct((G, MLA_HEADS, Sg, width), BF16),
                pl.BlockSpec((1, MLA_HEADS, tm, width), head_row))

    kpe_t = (jax.ShapeDtypeStruct((G, QK_ROPE, Sg), F32), pl.BlockSpec((1, QK_ROPE, tm), lambda g, i: (g, 0, i)))
    outs = [tok(KV_LORA, F32), kpe_t, tok(KV_LORA, BF16)]
    if absorb:
        outs += [tok(QK_ROPE, BF16), per_head(KV_LORA), per_head(QK_ROPE)]
    else:
        outs += [per_head(2 * QK_NOPE), per_head(2 * QK_NOPE)]
    ret_k = tok(D, act_dtype) if absorb else (jax.ShapeDtypeStruct((G, D, Sg), act_dtype),
                                              pl.BlockSpec((1, D, tm), lambda g, i: (g, 0, i)))
    outs += [tok(D, act_dtype), ret_k, tok(D, act_dtype), tok(D, BF16), tok(D, BF16)]
    return pl.pallas_call(
        functools.partial(_inproj_kernel, absorb=absorb),
        grid=(G, Sg // tm),
        in_specs=in_specs, out_specs=[o[1] for o in outs], out_shape=[o[0] for o in outs],
        compiler_params=_params(("parallel", "parallel")),
        name="inproj",
    )(x3, mod3, *tabs, w["w_head"], w["w_r"], w["g_q"], w["g_kv"],
      w["w_uq_nope"], w["w_uq_pe"], w_uk)


def _mla_prompt_kernel(qf_ref, kf_ref, v_ref, wuv_ref, o_ref,
                       m_ref, l_ref, acc_ref, s_ref, *, tq):
    qi = pl.program_id(1)

    def rows_of(j):
        return pl.ds(pl.multiple_of(j * tq, tq), tq)

    def scores(head, j):
        return _dot_nt(qf_ref[0, head], kf_ref[0, head, rows_of(j), :])

    for hd in range(SCORE_LOOKAHEAD):
        s_ref[hd] = scores(hd, 0)

    def tile(j, first, last):
        v = v_ref[0, rows_of(j), :]
        col = lax.broadcasted_iota(jnp.int32, (tq, tq), 1)
        row = lax.broadcasted_iota(jnp.int32, (tq, tq), 0)
        if first:
            keep = col <= row + qi * tq
        elif last:
            keep = col <= row
        pending = [s_ref[hd] for hd in range(SCORE_LOOKAHEAD)]
        for head in range(MLA_HEADS):
            s = pending.pop(0)
            ahead = head + SCORE_LOOKAHEAD
            if ahead < MLA_HEADS:
                pending.append(scores(ahead, j))
            elif not last:
                s_ref[ahead - MLA_HEADS] = scores(ahead - MLA_HEADS, j + 1)
            if first or last:
                s = jnp.where(keep, s, NEG)
            m_cur = jnp.max(s, axis=-1, keepdims=True)
            m_new = jnp.broadcast_to(m_cur, (tq, LANES)) if first else jnp.maximum(m_ref[head], m_cur)
            p = jnp.exp2(s - jnp.concatenate([m_new] * (tq // LANES), axis=1))
            psum = p[:, 0:LANES]
            for c in range(1, tq // LANES):
                psum = psum + p[:, c * LANES:(c + 1) * LANES]
            pv = _dot(p.astype(BF16), v)
            if first:
                l_ref[head] = psum
                acc_ref[head] = pv
            else:
                a = jnp.exp2(m_ref[head] - m_new)
                l_ref[head] = a * l_ref[head] + psum
                acc_ref[head] = jnp.concatenate([a] * (KV_LORA // LANES), axis=1) * acc_ref[head] + pv
            m_ref[head] = m_new

    def body(j, carry):
        tile(j, False, False)
        return carry

    tile(0, True, False)
    lax.fori_loop(1, qi, body, 0)

    @pl.when(qi > 0)
    def _():
        tile(qi, False, True)

    heads = range(MLA_HEADS)
    inv = [1.0 / jnp.sum(l_ref[head], axis=-1, keepdims=True) for head in heads]
    lat = [(acc_ref[head] * inv[head]).astype(BF16) for head in heads]
    out = [_dot(lat[head], wuv_ref[head]) for head in heads]
    for head in heads:
        o_ref[0, :, head * V_HEAD:(head + 1) * V_HEAD] = out[head].astype(BF16)


def _mla_prompt(qf, kf, v, w_uv, *, tq):
    B, Hh, S, width = qf.shape
    assert tq % LANES == 0 and S >= 2 * tq
    return pl.pallas_call(
        functools.partial(_mla_prompt_kernel, tq=tq),
        grid=(B, S // tq),
        in_specs=[pl.BlockSpec((1, Hh, tq, width), lambda b, i: (b, 0, i, 0)),
                  pl.BlockSpec((1, Hh, S, width), lambda b, i: (b, 0, 0, 0)),
                  pl.BlockSpec((1, S, KV_LORA), lambda b, i: (b, 0, 0)),
                  _resident(w_uv.shape)],
        out_specs=pl.BlockSpec((1, tq, D_MODEL), lambda b, i: (b, i, 0)),
        out_shape=jax.ShapeDtypeStruct((B, S, D_MODEL), BF16),
        scratch_shapes=[pltpu.VMEM((Hh, tq, LANES), F32), pltpu.VMEM((Hh, tq, LANES), F32),
                        pltpu.VMEM((Hh, tq, KV_LORA), F32), pltpu.VMEM((SCORE_LOOKAHEAD, tq, tq), F32)],
        compiler_params=_params(("parallel", "parallel")),
        name="mla_prompt",
    )(qf, kf, v, w_uv)


def _mla_sample_kernel(pt_ref, q1_ref, q2_ref, cn_ref, kn_ref, ck_hbm, kp_hbm, o_ref,
                       ck_buf, kp_buf, sems, *, pages, groups, tokens):
    b = pl.program_id(0)
    n_b = pl.num_programs(0)
    n_pages = pages * groups

    def page_copies(bb, i):
        slot = lax.rem(bb, 2)
        page = pt_ref[bb, i]
        return (pltpu.make_async_copy(ck_hbm.at[0, page], ck_buf.at[slot, i], sems.at[0, slot]),
                pltpu.make_async_copy(kp_hbm.at[0, page], kp_buf.at[slot, i], sems.at[1, slot]))

    def start_page(bb, i):
        for k, cp in enumerate(page_copies(bb, i)):
            cp.start(priority=(i + k) % 2)

    @pl.when(b == 0)
    def _():
        for i in range(n_pages):
            start_page(b, i)

    for i in range(n_pages):
        for cp in page_copies(b, i):
            cp.wait()

    def start_next_row(i):
        @pl.when(b + 1 < n_b)
        def _():
            start_page(b + 1, i)

    q1 = q1_ref[0]
    q2 = q2_ref[0]
    rows = q1.shape[0]
    slot = lax.rem(b, 2)

    def scores(g):
        cols = []
        for i in range(g * pages, (g + 1) * pages):
            cols.append(_dot_nt(q1, ck_buf[slot, i].astype(BF16)) + _dot(q2, kp_buf[slot, i].astype(BF16)))
            if i < n_pages // NEXT_ROW_BURST:
                for k in range(NEXT_ROW_BURST):
                    start_next_row(NEXT_ROW_BURST * i + k)
        return cols

    def update(g, cols, m, l, acc):
        mx = cols[0]
        for c in cols[1:]:
            mx = jnp.maximum(mx, c)
        m_new = jnp.maximum(m, jnp.max(mx, axis=-1, keepdims=True))
        a = jnp.exp2(m - m_new)
        ps = [jnp.exp2(c - m_new) for c in cols]
        psum = ps[0]
        for p in ps[1:]:
            psum = psum + p
        pv = None
        for j in range(pages):
            part = _dot(ps[j].astype(BF16), ck_buf[slot, g * pages + j].astype(BF16))
            pv = part if pv is None else pv + part
        acc = jnp.concatenate([a] * (KV_LORA // LANES), axis=1) * acc + pv
        return m_new, a * l + psum, acc

    m = jnp.full((rows, LANES), -jnp.inf, F32)
    l = jnp.zeros((rows, LANES), F32)
    acc = jnp.zeros((rows, KV_LORA), F32)
    pending = scores(0)
    for g in range(groups):
        nxt = scores(g + 1) if g + 1 < groups else None
        m, l, acc = update(g, pending, m, l, acc)
        pending = nxt

    q1f = q1.astype(F32)
    q2f = q2.astype(F32)
    cn = cn_ref[0]
    kn = kn_ref[0]
    tok = lax.broadcasted_iota(jnp.int32, (rows, LANES), 0) & (tokens - 1)
    cols = []
    for t in range(tokens):
        sc = (jnp.sum(q1f * cn[t:t + 1, :], axis=-1, keepdims=True)
              + jnp.sum(q2f * kn[t:t + 1, :], axis=-1, keepdims=True))
        cols.append(jnp.where(tok >= t, sc, NEG))
    m_fin = m
    for sc in cols:
        m_fin = jnp.maximum(m_fin, sc)
    a2 = jnp.exp2(m - m_fin)
    l_fin = jnp.sum(a2 * l, axis=-1, keepdims=True)
    acc = jnp.concatenate([a2] * (KV_LORA // LANES), axis=1) * acc
    for t, sc in enumerate(cols):
        pt = jnp.exp2(sc - m_fin)
        l_fin = l_fin + pt[:, 0:1]
        acc = acc + jnp.concatenate([pt] * (KV_LORA // LANES), axis=1) * cn[t:t + 1, :]
    o_ref[0] = acc * (1.0 / l_fin)


def _mla_sample(page_table, q1, q2, cnew, knew, cache_ckv, cache_kpe_t, *, pages):
    DB, n_pages = page_table.shape
    rows = q1.shape[1]
    tokens = cnew.shape[1]
    assert n_pages % pages == 0 and tokens & (tokens - 1) == 0
    groups = n_pages // pages

    per_batch = lambda b, pt: (b, 0, 0)
    grid_spec = pltpu.PrefetchScalarGridSpec(
        num_scalar_prefetch=1,
        grid=(DB,),
        in_specs=[pl.BlockSpec((1, rows, KV_LORA), per_batch),
                  pl.BlockSpec((1, rows, QK_ROPE), per_batch),
                  pl.BlockSpec((1, tokens, KV_LORA), per_batch),
                  pl.BlockSpec((1, tokens, QK_ROPE), per_batch),
                  pl.BlockSpec(memory_space=pl.ANY),
                  pl.BlockSpec(memory_space=pl.ANY)],
        out_specs=pl.BlockSpec((1, rows, KV_LORA), per_batch),
        scratch_shapes=[pltpu.VMEM((2, n_pages, PAGE_SIZE, KV_LORA), F32),
                        pltpu.VMEM((2, n_pages, QK_ROPE, PAGE_SIZE), F32),
                        pltpu.SemaphoreType.DMA((2, 2))],
    )
    return pl.pallas_call(
        functools.partial(_mla_sample_kernel, pages=pages, groups=groups, tokens=tokens),
        grid_spec=grid_spec,
        out_shape=jax.ShapeDtypeStruct((DB, rows, KV_LORA), F32),
        compiler_params=_params(("arbitrary",)),
        name="mla_sample",
    )(page_table, q1, q2, cnew, knew, cache_ckv, cache_kpe_t)


def _retention_kernel(*refs, T, L, has_s0):
    C = RET_CHUNK
    padded = T < C
    if has_s0:
        q_ref, k_ref, v_ref, s0_ref, y_ref, s_ref = refs[:6]
        rest = refs[6:]
    else:
        q_ref, k_ref, v_ref, y_ref, s_ref = refs[:5]
        rest = refs[5:]
    heads = range(RET_HEADS)
    ri = lax.broadcasted_iota(jnp.int32, (C, C), 0).astype(F32)
    ci = lax.broadcasted_iota(jnp.int32, (C, C), 1).astype(F32)
    diff = ri - ci
    causal = diff >= 0.0
    log_gamma = [math.log(1.0 - 2.0 ** (-5.0 - h)) for h in heads]
    dmask = [jnp.where(causal, jnp.exp(jnp.where(causal, diff, 0.0) * lg), 0.0) for lg in log_gamma]
    q_decay = [jnp.exp((ri + 1.0) * lg) for lg in log_gamma]
    k_decay = [jnp.exp((L - 1.0 - ci) * lg) for lg in log_gamma]
    chunk_decay = [math.exp(L * lg) for lg in log_gamma]

    def col(h):
        return slice(h * RET_DK, (h + 1) * RET_DK)

    def chunk(q, kt, v, store):
        s = [s_ref[0, h] for h in heads]
        scores = [_dot(q[h], kt[h].astype(BF16)) for h in heads]
        cross = [_dot(q[h], s[h].astype(BF16)) for h in heads]
        kd_t = [(kt[h].astype(F32) * k_decay[h]).astype(BF16) for h in heads]
        inner = [_dot((scores[h] * dmask[h]).astype(BF16), v[h]) for h in heads]
        upd = [_dot(kd_t[h], v[h]) for h in heads]
        for h in heads:
            store(h, inner[h] + cross[h] * q_decay[h])
            s_ref[0, h] = chunk_decay[h] * s[h] + upd[h]

    for h in heads:
        s_ref[0, h] = s0_ref[0, h] if has_s0 else jnp.zeros((RET_DK, RET_DV), F32)
    if padded:
        for src, dst in zip((q_ref, k_ref, v_ref), rest):
            dst[...] = jnp.zeros(dst.shape, F32)
            dst[0:T, :] = src[0].astype(F32)
        qp_ref, kp_ref, vp_ref = rest

        def store(h, y):
            y_ref[0, :, col(h)] = y[0:T, :].astype(y_ref.dtype)
        chunk([qp_ref[:, col(h)].astype(BF16) for h in heads],
              [kp_ref[:, col(h)].T for h in heads],
              [vp_ref[:, col(h)].astype(BF16) for h in heads], store)
    else:
        def body(c, carry):
            rows = pl.ds(pl.multiple_of(c * C, C), C)

            def store(h, y):
                y_ref[0, rows, col(h)] = y.astype(y_ref.dtype)
            chunk([q_ref[0, rows, col(h)] for h in heads],
                  [k_ref[0, col(h), rows] for h in heads],
                  [v_ref[0, rows, col(h)] for h in heads], store)
            return carry
        lax.fori_loop(0, T // C, body, 0)


def _retention(rq, rk, rv, s0, *, L, out_dtype):
    B, T, D = rq.shape
    has_s0 = s0 is not None
    padded = T < RET_CHUNK
    assert padded or T % RET_CHUNK == 0
    tok = pl.BlockSpec((1, T, D), lambda b: (b, 0, 0))
    key = tok if padded else pl.BlockSpec((1, D, T), lambda b: (b, 0, 0))
    state = pl.BlockSpec((1, RET_HEADS, RET_DK, RET_DV), lambda b: (b, 0, 0, 0))
    in_specs = [tok, key, tok] + ([state] if has_s0 else [])
    args = (rq, rk, rv) + ((s0,) if has_s0 else ())
    return pl.pallas_call(
        functools.partial(_retention_kernel, T=T, L=L, has_s0=has_s0),
        grid=(B,),
        in_specs=in_specs,
        out_specs=[tok, state],
        out_shape=[jax.ShapeDtypeStruct((B, T, D), out_dtype),
                   jax.ShapeDtypeStruct((B, RET_HEADS, RET_DK, RET_DV), F32)],
        scratch_shapes=[pltpu.VMEM((RET_CHUNK, D), F32)] * 3 if padded else [],
        compiler_params=_params(("parallel",)),
        name="retention",
    )(*args)


def _post_kernel(*refs, apply_uv):
    if apply_uv:
        (x_ref, att_ref, y_ref, ga_ref, gr_ref, mod_ref, wuv_ref, wo_ref, wup_ref, wdn_ref,
         g1_ref, b1_ref, g2_ref, b2_ref, o_ref) = refs
    else:
        (x_ref, att_ref, y_ref, ga_ref, gr_ref, mod_ref, wo_ref, wup_ref, wdn_ref,
         g1_ref, b1_ref, g2_ref, b2_ref, o_ref) = refs
    D = D_MODEL
    tm = x_ref.shape[1]
    n_sub = POST_SPLIT if tm % (POST_SPLIT * 16) == 0 else 1
    subs = [slice(i * (tm // n_sub), (i + 1) * (tm // n_sub)) for i in range(n_sub)]

    def mod(rows, k):
        cols = slice(k * D, (k + 1) * D)
        return mod_ref[0, :, cols] if mod_ref.shape[1] == 1 else mod_ref[0, rows, cols]

    def mix_in(rows):
        if apply_uv:
            o_a = jnp.concatenate([_dot(att_ref[0, head, rows, :].astype(BF16), wuv_ref[head])
                                   for head in range(MLA_HEADS)], axis=1)
        else:
            o_a = att_ref[0, rows, :].astype(F32)
        y = y_ref[0, rows, :].astype(F32)
        normed = []
        for head in range(RET_HEADS):
            yh = y[:, head * RET_DV:(head + 1) * RET_DV]
            yc = yh - jnp.mean(yh, axis=-1, keepdims=True)
            normed.append(yc * lax.rsqrt(jnp.mean(yc * yc, axis=-1, keepdims=True) + LN_EPS))
        mixv = (ga_ref[0, rows, :].astype(F32) * o_a
                + gr_ref[0, rows, :].astype(F32) * jnp.concatenate(normed, axis=1))
        return mixv.astype(BF16)

    mixv = [mix_in(rows) for rows in subs]
    mix = [_dot(mv, wo_ref[...]) for mv in mixv]
    x1 = [_layer_norm(ALPHA * x_ref[0, rows, :] + (1.0 + mod(rows, 2)) * mx, g1_ref[...], b1_ref[...])
          for rows, mx in zip(subs, mix)]
    h2 = [(x * (1.0 + mod(rows, 4)) + mod(rows, 3)).astype(BF16) for rows, x in zip(subs, x1)]

    def up(unit):
        i, c = unit
        return _dot(h2[i], wup_ref[:, c * D:(c + 1) * D])

    units = [(i, c) for i in range(n_sub) for c in range(D_FF // D)]
    m = [None] * n_sub
    u_next = up(units[0])
    for idx, (i, c) in enumerate(units):
        u = jnp.maximum(u_next, 0.0)
        if idx + 1 < len(units):
            u_next = up(units[idx + 1])
        part = _dot((u * u).astype(BF16), wdn_ref[c * D:(c + 1) * D, :])
        m[i] = part if m[i] is None else m[i] + part
    for i, rows in enumerate(subs):
        o_ref[0, rows, :] = _layer_norm(ALPHA * x1[i] + (1.0 + mod(rows, 5)) * m[i], g2_ref[...], b2_ref[...])


def _post(x3, att, y3, ga, gr, mod3, w, *, tm, apply_uv):
    G, Sg, D = x3.shape
    R = mod3.shape[1]
    row = lambda g, i: (g, i, 0)
    act = pl.BlockSpec((1, tm, D), row)
    if apply_uv:
        att_spec = pl.BlockSpec((1, MLA_HEADS, tm, KV_LORA), lambda g, i: (g, 0, i, 0))
    else:
        att_spec = act
    in_specs = [act, att_spec, act, act, act, pl.BlockSpec((1, R, 6 * D), lambda g, i: (g, 0, 0))]
    args = [x3, att, y3, ga, gr, mod3]
    names = (["w_uv"] if apply_uv else []) + ["w_o", "w_up", "w_down", "ln1_g", "ln1_b", "ln2_g", "ln2_b"]
    for n in names:
        in_specs.append(_resident(w[n].shape))
        args.append(w[n])
    return pl.pallas_call(
        functools.partial(_post_kernel, apply_uv=apply_uv),
        grid=(G, Sg // tm),
        in_specs=in_specs,
        out_specs=act,
        out_shape=jax.ShapeDtypeStruct((G, Sg, D), F32),
        compiler_params=_params(("parallel", "parallel")),
        name="post",
    )(*args)


def _rope_tables(pos, dim):
    half = dim // 2
    inv = ROPE_BASE ** (-jnp.arange(half, dtype=F32) / half)
    ang = pos.astype(F32)[:, None] * inv[None, :]
    cos, sin = jnp.cos(ang), jnp.sin(ang)
    reps = LANES // dim
    return (jnp.tile(jnp.concatenate([cos, cos], -1), (1, reps)),
            jnp.tile(jnp.concatenate([-sin, sin], -1), (1, reps)))


def kernel(x_prompt, x_sample, c_prompt, c_sample, cache_ckv, cache_kpe, state_ret, page_table,
           w_ada, b_ada, w_in, g_qnorm, g_kvnorm, w_uq, w_uk, w_uv, w_o,
           ln1_g, ln1_b, w_up, w_down, ln2_g, ln2_b):
    B, S, D = x_prompt.shape
    DB, T, _ = x_sample.shape
    n_pages = page_table.shape[1]
    past_len = n_pages * PAGE_SIZE
    assert DEPTH == 1 and w_in.shape[0] == 1
    l = 0
    n_head = Q_LORA + KV_LORA + QK_ROPE

    w_in_t = jnp.swapaxes(w_in[l], 0, 1)
    w = {
        "w_head": jnp.concatenate([w_in_t[:n_head], w_in_t[Q_LORA + KV_LORA:n_head]], axis=0).astype(BF16),
        "w_r": w_in_t[n_head:].astype(BF16),
        "g_q": g_qnorm[l][None, :], "g_kv": g_kvnorm[l][None, :],
        "w_uq_nope": w_uq[l][:, :, :QK_NOPE].reshape(Q_LORA, MLA_HEADS * QK_NOPE).astype(BF16),
        "w_uq_pe": w_uq[l][:, :, QK_NOPE:].reshape(Q_LORA, MLA_HEADS * QK_ROPE).astype(BF16),
        "w_ukT": jnp.transpose(w_uk[l], (1, 2, 0)).astype(BF16),
        "w_uk_flat": w_uk[l].reshape(KV_LORA, MLA_HEADS * QK_NOPE).astype(BF16),
        "w_uv": jnp.transpose(w_uv[l], (1, 0, 2)).astype(BF16),
        "w_o": w_o[l].astype(BF16), "w_up": w_up[l].astype(BF16), "w_down": w_down[l].astype(BF16),
        "ln1_g": ln1_g[l][None, :], "ln1_b": ln1_b[l][None, :],
        "ln2_g": ln2_g[l][None, :], "ln2_b": ln2_b[l][None, :],
    }

    n_c = B + DB
    pad = (-n_c) % 16
    c_all = jnp.concatenate([c_prompt, c_sample, jnp.zeros((pad, D), F32)], axis=0)
    ada = _ada(c_all, w_ada[l], b_ada[l][None, :])
    mod_p = ada[:B].reshape(B, 1, 6 * D)
    mod_s = jnp.repeat(ada[B:n_c], T, axis=0).reshape(1, DB * T, 6 * D)

    pos_p = jnp.arange(S)
    tabs_p = _rope_tables(pos_p, QK_ROPE) + _rope_tables(pos_p, RET_DK)
    (ckv_p, kpe_p, kc_p, qf_p, kf_p, rq_p, rk_p, rv_p, ga_p, gr_p) = _inproj(
        x_prompt, mod_p, tabs_p, w, tm=512, act_dtype=BF16, absorb=False)
    kpe_p = jnp.swapaxes(kpe_p, 1, 2)
    oa_p = _mla_prompt(qf_p, kf_p, kc_p, w["w_uv"], tq=256)
    y_p, ret_p = _retention(rq_p, rk_p, rv_p, None, L=min(RET_CHUNK, S), out_dtype=BF16)
    out_p = _post(x_prompt, oa_p, y_p, ga_p, gr_p, mod_p, w, tm=512, apply_uv=False)

    n_s = DB * T
    pos_s = jnp.tile(past_len + jnp.arange(T), DB)
    tabs_s = _rope_tables(pos_s, QK_ROPE) + _rope_tables(pos_s, RET_DK)
    xs = x_sample.reshape(1, n_s, D)
    (ckv_s, kpe_s, _, _, qlat_s, qpe_s, rq_s, rk_s, rv_s, ga_s, gr_s) = _inproj(
        xs, mod_s, tabs_s, w, tm=n_s, act_dtype=F32, absorb=True)
    kpe_s = jnp.swapaxes(kpe_s, 1, 2)

    def to_batch_rows(q):
        d = q.shape[-1]
        return q[0].reshape(MLA_HEADS, DB, T, d).transpose(1, 0, 2, 3).reshape(DB, MLA_HEADS * T, d)

    olat_s = _mla_sample(page_table, to_batch_rows(qlat_s), to_batch_rows(qpe_s),
                         ckv_s.reshape(DB, T, KV_LORA), kpe_s.reshape(DB, T, QK_ROPE),
                         cache_ckv, jnp.swapaxes(cache_kpe, 2, 3), pages=16)
    olat_s = olat_s.reshape(DB, MLA_HEADS, T, KV_LORA).transpose(1, 0, 2, 3).reshape(1, MLA_HEADS, n_s, KV_LORA)
    y_s, ret_s = _retention(rq_s.reshape(DB, T, D), rk_s.reshape(DB, T, D), rv_s.reshape(DB, T, D),
                            state_ret[l], L=T, out_dtype=F32)
    out_s = _post(xs, olat_s, y_s.reshape(1, n_s, D), ga_s, gr_s, mod_s, w, tm=n_s, apply_uv=True)

    return (out_p, out_s.reshape(DB, T, D),
            ckv_p[None], kpe_p[None], ret_p[None],
            ckv_s.reshape(1, DB, T, KV_LORA), kpe_s.reshape(1, DB, T, QK_ROPE), ret_s[None])
```

```python
import functools
import math

import jax
import jax.numpy as jnp
from jax import lax
from jax.experimental import pallas as pl
from jax.experimental.pallas import tpu as pltpu

D_MODEL = 1024
DEPTH = 1
PAGE_SIZE = 128
MLA_HEADS = 8
Q_LORA = 384
KV_LORA = 256
QK_NOPE = 128
QK_ROPE = 64
V_HEAD = D_MODEL // MLA_HEADS
MLA_SCALE = (QK_NOPE + QK_ROPE) ** -0.5
LOG2E = 1.4426950408889634
Q_SCALE = MLA_SCALE * LOG2E
RET_HEADS = 8
RET_DK = 128
RET_DV = D_MODEL // RET_HEADS
RET_CHUNK = 128
D_FF = 4 * D_MODEL
ROPE_BASE = 10000.0
LN_EPS = 1e-5
RMS_EPS = 1e-6
NEG = -1e30
ALPHA = (2.0 * DEPTH) ** 0.25

SCORE_LOOKAHEAD = 3
POST_SPLIT = 2
RET_CHUNKS_PER_TRIP = 2
NEXT_ROW_BURST = 2
LANES = 128
V7X_VMEM_BYTES = 64 * 1024 * 1024
VMEM_LIMIT = V7X_VMEM_BYTES - 8 * 1024 * 1024

F32 = jnp.float32
BF16 = jnp.bfloat16


def _params(sem):
    return pltpu.CompilerParams(dimension_semantics=sem, vmem_limit_bytes=VMEM_LIMIT)


def _resident(shape):
    nd = len(shape)
    return pl.BlockSpec(shape, lambda *_: (0,) * nd, pipeline_mode=pl.Buffered(1))


def _dot(a, b):
    return jnp.dot(a, b, preferred_element_type=F32)


def _dot_nt(a, b):
    return lax.dot_general(a, b, (((1,), (1,)), ((), ())), preferred_element_type=F32)


def _rope(x, cos, sin_signed, half):
    lanes = x.shape[-1]
    if 2 * half == lanes:
        partner = pltpu.roll(x, half, axis=1)
    else:
        lane = lax.broadcasted_iota(jnp.int32, x.shape, 1)
        first = (lane & (2 * half - 1)) < half
        partner = jnp.where(first, pltpu.roll(x, lanes - half, axis=1), pltpu.roll(x, half, axis=1))
    return x * cos + partner * sin_signed


def _layer_norm(v, g, b):
    mu = jnp.mean(v, axis=-1, keepdims=True)
    vc = v - mu
    var = jnp.mean(vc * vc, axis=-1, keepdims=True)
    return vc * lax.rsqrt(var + LN_EPS) * g + b


def _ada_kernel(c_ref, w_ref, b_ref, o_ref):
    c = c_ref[...]
    s = (c * jax.nn.sigmoid(c)).astype(BF16)
    o_ref[...] = _dot(s, w_ref[...].astype(BF16)) + b_ref[...]


def _ada(c_all, w_ada, b_ada):
    rows = c_all.shape[0]
    tn = D_MODEL
    return pl.pallas_call(
        _ada_kernel,
        grid=(6 * D_MODEL // tn,),
        in_specs=[pl.BlockSpec((rows, D_MODEL), lambda j: (0, 0)),
                  pl.BlockSpec((D_MODEL, tn), lambda j: (0, j)),
                  pl.BlockSpec((1, tn), lambda j: (0, j))],
        out_specs=pl.BlockSpec((rows, tn), lambda j: (0, j)),
        out_shape=jax.ShapeDtypeStruct((rows, 6 * D_MODEL), F32),
        compiler_params=_params(("parallel",)),
        name="ada",
    )(c_all, w_ada, b_ada)


def _inproj_kernel(x_ref, mod_ref, cosa_ref, sina_ref, cosb_ref, sinb_ref,
                   wh_ref, wr_ref, gq_ref, gkv_ref, wqn_ref, wqp_ref, wuk_ref,
                   ckv_ref, kpe_ref, kc_ref, *refs, absorb):
    if absorb:
        kp_ref, qlat_ref, qpe_ref, rq_ref, rk_ref, rv_ref, ga_ref, gr_ref = refs
    else:
        qf_ref, kf_ref, rq_ref, rk_ref, rv_ref, ga_ref, gr_ref = refs
    D = D_MODEL
    x = x_ref[0]
    sh1 = mod_ref[0, :, 0:D]
    sc1 = mod_ref[0, :, D:2 * D]
    h = (x * (1.0 + sc1) + sh1).astype(BF16)

    yh = _dot_nt(h, wh_ref[...])
    pq = yh[:, 0:Q_LORA]
    pkv = yh[:, Q_LORA:Q_LORA + KV_LORA]
    pkr = yh[:, Q_LORA + KV_LORA:Q_LORA + KV_LORA + LANES]
    qn = (pq * lax.rsqrt(jnp.mean(pq * pq, axis=-1, keepdims=True) + RMS_EPS) * gq_ref[...]).astype(BF16)
    ckv = pkv * lax.rsqrt(jnp.mean(pkv * pkv, axis=-1, keepdims=True) + RMS_EPS) * gkv_ref[...]
    ckv_ref[0] = ckv
    kc_ref[0] = ckv.astype(BF16)
    cosa = cosa_ref[...]
    sina = sina_ref[...]
    kpe2 = _rope(pkr, cosa, sina, QK_ROPE // 2)
    kpe_ref[0] = kpe2.T[0:QK_ROPE, :]

    qpe = _dot(qn, wqp_ref[...])
    qnope = _dot(qn, wqn_ref[...])
    pe_blocks = [_rope(qpe[:, j * LANES:(j + 1) * LANES], cosa, sina, QK_ROPE // 2) * Q_SCALE
                 for j in range(MLA_HEADS * QK_ROPE // LANES)]
    if absorb:
        kp_ref[0] = kpe2[:, 0:QK_ROPE].astype(BF16)
        for head in range(MLA_HEADS):
            off = (head % 2) * QK_ROPE
            qpe_ref[0, head] = pe_blocks[head // 2][:, off:off + QK_ROPE].astype(BF16)
            qh = qnope[:, head * QK_NOPE:(head + 1) * QK_NOPE].astype(BF16)
            qlat_ref[0, head] = (_dot(qh, wuk_ref[head]) * Q_SCALE).astype(BF16)
    else:
        lane = lax.broadcasted_iota(jnp.int32, kpe2.shape, 1)
        kpe_pad = jnp.where(lane < QK_ROPE, kpe2, 0.0).astype(BF16)
        knope = _dot(ckv.astype(BF16), wuk_ref[...])
        for head in range(MLA_HEADS):
            sl = slice(head * QK_NOPE, (head + 1) * QK_NOPE)
            kf_ref[0, head, :, 0:QK_NOPE] = knope[:, sl].astype(BF16)
            kf_ref[0, head, :, QK_NOPE:2 * QK_NOPE] = kpe_pad
            qf_ref[0, head, :, 0:QK_NOPE] = (qnope[:, sl] * Q_SCALE).astype(BF16)
            blk = pe_blocks[head // 2]
            if head % 2:
                blk = pltpu.roll(blk, QK_ROPE, axis=1)
            qf_ref[0, head, :, QK_NOPE:2 * QK_NOPE] = blk.astype(BF16)

    cosb = cosb_ref[...]
    sinb = sinb_ref[...]
    rq = _dot_nt(h, wr_ref[0:D, :])
    rk = _dot_nt(h, wr_ref[D:2 * D, :])
    for head in range(RET_HEADS):
        sl = slice(head * RET_DK, (head + 1) * RET_DK)
        rq_ref[0, :, sl] = _rope(rq[:, sl], cosb, sinb, RET_DK // 2).astype(rq_ref.dtype)
        rk_h = _rope(rk[:, sl], cosb, sinb, RET_DK // 2) * (RET_DK ** -0.5)
        if absorb:
            rk_ref[0, :, sl] = rk_h.astype(rk_ref.dtype)
        else:
            rk_ref[0, sl, :] = rk_h.T.astype(rk_ref.dtype)
    rv_ref[0] = _dot_nt(h, wr_ref[2 * D:3 * D, :]).astype(rv_ref.dtype)
    rg = _dot_nt(h, wr_ref[3 * D:4 * D, :])
    ga_ref[0] = jax.nn.sigmoid(_dot_nt(h, wr_ref[4 * D:5 * D, :])).astype(BF16)
    gr = jax.nn.sigmoid(_dot_nt(h, wr_ref[5 * D:6 * D, :]))
    gr_ref[0] = (gr * (rg * jax.nn.sigmoid(rg))).astype(BF16)


def _inproj(x3, mod3, tabs, w, *, tm, act_dtype, absorb):
    G, Sg, D = x3.shape
    R = mod3.shape[1]
    assert Sg % tm == 0 and R in (1, tm) and (R == 1 or Sg == tm)
    row = lambda g, i: (g, i, 0)
    head_row = lambda g, i: (g, 0, i, 0)
    tab = pl.BlockSpec((tm, LANES), lambda g, i: (i, 0))
    w_uk = w["w_ukT"] if absorb else w["w_uk_flat"]
    in_specs = [
        pl.BlockSpec((1, tm, D), row),
        pl.BlockSpec((1, R, 2 * D), lambda g, i: (g, 0, 0)),
        tab, tab, tab, tab,
        _resident(w["w_head"].shape), _resident(w["w_r"].shape),
        _resident(w["g_q"].shape), _resident(w["g_kv"].shape),
        _resident(w["w_uq_nope"].shape), _resident(w["w_uq_pe"].shape), _resident(w_uk.shape),
    ]

    def tok(width, dtype):
        return jax.ShapeDtypeStruct((G, Sg, width), dtype), pl.BlockSpec((1, tm, width), row)

    def per_head(width):
        return (jax.ShapeDtypeStruct((G, MLA_HEADS, Sg, width), BF16),
                pl.BlockSpec((1, MLA_HEADS, tm, width), head_row))

    kpe_t = (jax.ShapeDtypeStruct((G, QK_ROPE, Sg), F32), pl.BlockSpec((1, QK_ROPE, tm), lambda g, i: (g, 0, i)))
    outs = [tok(KV_LORA, F32), kpe_t, tok(KV_LORA, BF16)]
    if absorb:
        outs += [tok(QK_ROPE, BF16), per_head(KV_LORA), per_head(QK_ROPE)]
    else:
        outs += [per_head(2 * QK_NOPE), per_head(2 * QK_NOPE)]
    ret_k = tok(D, act_dtype) if absorb else (jax.ShapeDtypeStruct((G, D, Sg), act_dtype),
                                              pl.BlockSpec((1, D, tm), lambda g, i: (g, 0, i)))
    outs += [tok(D, act_dtype), ret_k, tok(D, act_dtype), tok(D, BF16), tok(D, BF16)]
    return pl.pallas_call(
        functools.partial(_inproj_kernel, absorb=absorb),
        grid=(G, Sg // tm),
        in_specs=in_specs, out_specs=[o[1] for o in outs], out_shape=[o[0] for o in outs],
        compiler_params=_params(("parallel", "parallel")),
        name="inproj",
    )(x3, mod3, *tabs, w["w_head"], w["w_r"], w["g_q"], w["g_kv"],
      w["w_uq_nope"], w["w_uq_pe"], w_uk)


def _mla_prompt_kernel(qf_ref, kf_ref, v_ref, wuv_ref, o_ref,
                       m_ref, l_ref, acc_ref, s_ref, *, tq):
    qi = pl.program_id(1)

    def rows_of(j):
        return pl.ds(pl.multiple_of(j * tq, tq), tq)

    def scores(head, j):
        return _dot_nt(qf_ref[0, head], kf_ref[0, head, rows_of(j), :])

    for hd in range(SCORE_LOOKAHEAD):
        s_ref[hd] = scores(hd, 0)

    def tile(j, first, last):
        v = v_ref[0, rows_of(j), :]
        col = lax.broadcasted_iota(jnp.int32, (tq, tq), 1)
        row = lax.broadcasted_iota(jnp.int32, (tq, tq), 0)
        if first:
            keep = col <= row + qi * tq
        elif last:
            keep = col <= row
        pending = [s_ref[hd] for hd in range(SCORE_LOOKAHEAD)]
        for head in range(MLA_HEADS):
            s = pending.pop(0)
            ahead = head + SCORE_LOOKAHEAD
            if ahead < MLA_HEADS:
                pending.append(scores(ahead, j))
            elif not last:
                s_ref[ahead - MLA_HEADS] = scores(ahead - MLA_HEADS, j + 1)
            if first or last:
                s = jnp.where(keep, s, NEG)
            m_cur = jnp.max(s, axis=-1, keepdims=True)
            m_new = jnp.broadcast_to(m_cur, (tq, LANES)) if first else jnp.maximum(m_ref[head], m_cur)
            p = jnp.exp2(s - jnp.concatenate([m_new] * (tq // LANES), axis=1))
            psum = p[:, 0:LANES]
            for c in range(1, tq // LANES):
                psum = psum + p[:, c * LANES:(c + 1) * LANES]
            pv = _dot(p.astype(BF16), v)
            if first:
                l_ref[head] = psum
                acc_ref[head] = pv
            else:
                a = jnp.exp2(m_ref[head] - m_new)
                l_ref[head] = a * l_ref[head] + psum
                acc_ref[head] = jnp.concatenate([a] * (KV_LORA // LANES), axis=1) * acc_ref[head] + pv
            m_ref[head] = m_new

    n_mid = jnp.maximum(qi - 1, 0)

    def pair(i, carry):
        tile(1 + 2 * i, False, False)
        tile(2 + 2 * i, False, False)
        return carry

    tile(0, True, False)
    lax.fori_loop(0, n_mid >> 1, pair, 0)

    @pl.when((n_mid & 1) == 1)
    def _():
        tile(qi - 1, False, False)

    @pl.when(qi > 0)
    def _():
        tile(qi, False, True)

    heads = range(MLA_HEADS)
    inv = [1.0 / jnp.sum(l_ref[head], axis=-1, keepdims=True) for head in heads]
    lat = [(acc_ref[head] * inv[head]).astype(BF16) for head in heads]
    out = [_dot(lat[head], wuv_ref[head]) for head in heads]
    for head in heads:
        o_ref[0, :, head * V_HEAD:(head + 1) * V_HEAD] = out[head].astype(BF16)


def _mla_prompt(qf, kf, v, w_uv, *, tq):
    B, Hh, S, width = qf.shape
    assert tq % LANES == 0 and S >= 2 * tq
    return pl.pallas_call(
        functools.partial(_mla_prompt_kernel, tq=tq),
        grid=(B, S // tq),
        in_specs=[pl.BlockSpec((1, Hh, tq, width), lambda b, i: (b, 0, i, 0)),
                  pl.BlockSpec((1, Hh, S, width), lambda b, i: (b, 0, 0, 0)),
                  pl.BlockSpec((1, S, KV_LORA), lambda b, i: (b, 0, 0)),
                  _resident(w_uv.shape)],
        out_specs=pl.BlockSpec((1, tq, D_MODEL), lambda b, i: (b, i, 0)),
        out_shape=jax.ShapeDtypeStruct((B, S, D_MODEL), BF16),
        scratch_shapes=[pltpu.VMEM((Hh, tq, LANES), F32), pltpu.VMEM((Hh, tq, LANES), F32),
                        pltpu.VMEM((Hh, tq, KV_LORA), F32), pltpu.VMEM((SCORE_LOOKAHEAD, tq, tq), F32)],
        compiler_params=_params(("parallel", "parallel")),
        name="mla_prompt",
    )(qf, kf, v, w_uv)


def _mla_sample_kernel(pt_ref, q1_ref, q2_ref, cn_ref, kn_ref, ck_hbm, kp_hbm, o_ref,
                       ck_buf, kp_buf, sems, *, pages, groups, tokens):
    b = pl.program_id(0)
    n_b = pl.num_programs(0)
    n_pages = pages * groups

    def page_copies(bb, i):
        slot = lax.rem(bb, 2)
        page = pt_ref[bb, i]
        return (pltpu.make_async_copy(ck_hbm.at[0, page], ck_buf.at[slot, i], sems.at[0, slot]),
                pltpu.make_async_copy(kp_hbm.at[0, page], kp_buf.at[slot, i], sems.at[1, slot]))

    def start_page(bb, i):
        for k, cp in enumerate(page_copies(bb, i)):
            cp.start(priority=(i + k) % 2)

    @pl.when(b == 0)
    def _():
        for i in range(n_pages):
            start_page(b, i)

    for i in range(n_pages):
        for cp in page_copies(b, i):
            cp.wait()

    def start_next_row(i):
        @pl.when(b + 1 < n_b)
        def _():
            start_page(b + 1, i)

    q1 = q1_ref[0]
    q2 = q2_ref[0]
    rows = q1.shape[0]
    slot = lax.rem(b, 2)

    def scores(g):
        cols = []
        for i in range(g * pages, (g + 1) * pages):
            cols.append(_dot_nt(q1, ck_buf[slot, i].astype(BF16)) + _dot(q2, kp_buf[slot, i].astype(BF16)))
            if i < n_pages // NEXT_ROW_BURST:
                for k in range(NEXT_ROW_BURST):
                    start_next_row(NEXT_ROW_BURST * i + k)
        return cols

    def update(g, cols, m, l, acc):
        mx = cols[0]
        for c in cols[1:]:
            mx = jnp.maximum(mx, c)
        m_new = jnp.maximum(m, jnp.max(mx, axis=-1, keepdims=True))
        a = jnp.exp2(m - m_new)
        ps = [jnp.exp2(c - m_new) for c in cols]
        psum = ps[0]
        for p in ps[1:]:
            psum = psum + p
        pv = None
        for j in range(pages):
            part = _dot(ps[j].astype(BF16), ck_buf[slot, g * pages + j].astype(BF16))
            pv = part if pv is None else pv + part
        acc = jnp.concatenate([a] * (KV_LORA // LANES), axis=1) * acc + pv
        return m_new, a * l + psum, acc

    m = jnp.full((rows, LANES), -jnp.inf, F32)
    l = jnp.zeros((rows, LANES), F32)
    acc = jnp.zeros((rows, KV_LORA), F32)
    pending = scores(0)
    for g in range(groups):
        nxt = scores(g + 1) if g + 1 < groups else None
        m, l, acc = update(g, pending, m, l, acc)
        pending = nxt

    q1f = q1.astype(F32)
    q2f = q2.astype(F32)
    cn = cn_ref[0]
    kn = kn_ref[0]
    tok = lax.broadcasted_iota(jnp.int32, (rows, LANES), 0) & (tokens - 1)
    cols = []
    for t in range(tokens):
        sc = (jnp.sum(q1f * cn[t:t + 1, :], axis=-1, keepdims=True)
              + jnp.sum(q2f * kn[t:t + 1, :], axis=-1, keepdims=True))
        cols.append(jnp.where(tok >= t, sc, NEG))
    m_fin = m
    for sc in cols:
        m_fin = jnp.maximum(m_fin, sc)
    a2 = jnp.exp2(m - m_fin)
    l_fin = jnp.sum(a2 * l, axis=-1, keepdims=True)
    acc = jnp.concatenate([a2] * (KV_LORA // LANES), axis=1) * acc
    for t, sc in enumerate(cols):
        pt = jnp.exp2(sc - m_fin)
        l_fin = l_fin + pt[:, 0:1]
        acc = acc + jnp.concatenate([pt] * (KV_LORA // LANES), axis=1) * cn[t:t + 1, :]
    o_ref[0] = acc * (1.0 / l_fin)


def _mla_sample(page_table, q1, q2, cnew, knew, cache_ckv, cache_kpe_t, *, pages):
    DB, n_pages = page_table.shape
    rows = q1.shape[1]
    tokens = cnew.shape[1]
    assert n_pages % pages == 0 and tokens & (tokens - 1) == 0
    groups = n_pages // pages

    per_batch = lambda b, pt: (b, 0, 0)
    grid_spec = pltpu.PrefetchScalarGridSpec(
        num_scalar_prefetch=1,
        grid=(DB,),
        in_specs=[pl.BlockSpec((1, rows, KV_LORA), per_batch),
                  pl.BlockSpec((1, rows, QK_ROPE), per_batch),
                  pl.BlockSpec((1, tokens, KV_LORA), per_batch),
                  pl.BlockSpec((1, tokens, QK_ROPE), per_batch),
                  pl.BlockSpec(memory_space=pl.ANY),
                  pl.BlockSpec(memory_space=pl.ANY)],
        out_specs=pl.BlockSpec((1, rows, KV_LORA), per_batch),
        scratch_shapes=[pltpu.VMEM((2, n_pages, PAGE_SIZE, KV_LORA), F32),
                        pltpu.VMEM((2, n_pages, QK_ROPE, PAGE_SIZE), F32),
                        pltpu.SemaphoreType.DMA((2, 2))],
    )
    return pl.pallas_call(
        functools.partial(_mla_sample_kernel, pages=pages, groups=groups, tokens=tokens),
        grid_spec=grid_spec,
        out_shape=jax.ShapeDtypeStruct((DB, rows, KV_LORA), F32),
        compiler_params=_params(("arbitrary",)),
        name="mla_sample",
    )(page_table, q1, q2, cnew, knew, cache_ckv, cache_kpe_t)


def _retention_kernel(*refs, T, L, has_s0):
    C = RET_CHUNK
    padded = T < C
    if has_s0:
        q_ref, k_ref, v_ref, s0_ref, y_ref, s_ref = refs[:6]
        rest = refs[6:]
    else:
        q_ref, k_ref, v_ref, y_ref, s_ref = refs[:5]
        rest = refs[5:]
    heads = range(RET_HEADS)
    ri = lax.broadcasted_iota(jnp.int32, (C, C), 0).astype(F32)
    ci = lax.broadcasted_iota(jnp.int32, (C, C), 1).astype(F32)
    diff = ri - ci
    causal = diff >= 0.0
    log_gamma = [math.log(1.0 - 2.0 ** (-5.0 - h)) for h in heads]
    dmask = [jnp.where(causal, jnp.exp(jnp.where(causal, diff, 0.0) * lg), 0.0) for lg in log_gamma]
    q_decay = [jnp.exp((ri + 1.0) * lg) for lg in log_gamma]
    k_decay = [jnp.exp((L - 1.0 - ci) * lg) for lg in log_gamma]
    chunk_decay = [math.exp(L * lg) for lg in log_gamma]

    def col(h):
        return slice(h * RET_DK, (h + 1) * RET_DK)

    def chunks(items):
        scores = [[_dot(q[h], kt[h].astype(BF16)) for h in heads] for q, kt, v, _ in items]
        kd_t = [[(kt[h].astype(F32) * k_decay[h]).astype(BF16) for h in heads] for q, kt, v, _ in items]
        inner = [[_dot((sc[h] * dmask[h]).astype(BF16), v[h]) for h in heads]
                 for sc, (q, kt, v, _) in zip(scores, items)]
        s = [s_ref[0, h] for h in heads]
        for (q, kt, v, store), inn, kd in zip(items, inner, kd_t):
            cross = [_dot(q[h], s[h].astype(BF16)) for h in heads]
            upd = [_dot(kd[h], v[h]) for h in heads]
            for h in heads:
                store(h, inn[h] + cross[h] * q_decay[h])
            s = [chunk_decay[h] * s[h] + upd[h] for h in heads]
        for h in heads:
            s_ref[0, h] = s[h]

    for h in heads:
        s_ref[0, h] = s0_ref[0, h] if has_s0 else jnp.zeros((RET_DK, RET_DV), F32)
    if padded:
        for src, dst in zip((q_ref, k_ref, v_ref), rest):
            dst[...] = jnp.zeros(dst.shape, F32)
            dst[0:T, :] = src[0].astype(F32)
        qp_ref, kp_ref, vp_ref = rest

        def store(h, y):
            y_ref[0, :, col(h)] = y[0:T, :].astype(y_ref.dtype)
        chunks([([qp_ref[:, col(h)].astype(BF16) for h in heads],
                 [kp_ref[:, col(h)].T for h in heads],
                 [vp_ref[:, col(h)].astype(BF16) for h in heads], store)])
    else:
        per_trip = RET_CHUNKS_PER_TRIP if (T // C) % RET_CHUNKS_PER_TRIP == 0 else 1

        def item(c):
            rows = pl.ds(pl.multiple_of(c * C, C), C)

            def store(h, y):
                y_ref[0, rows, col(h)] = y.astype(y_ref.dtype)
            return ([q_ref[0, rows, col(h)] for h in heads],
                    [k_ref[0, col(h), rows] for h in heads],
                    [v_ref[0, rows, col(h)] for h in heads], store)

        def body(i, carry):
            chunks([item(i * per_trip + k) for k in range(per_trip)])
            return carry
        lax.fori_loop(0, T // C // per_trip, body, 0)


def _retention(rq, rk, rv, s0, *, L, out_dtype):
    B, T, D = rq.shape
    has_s0 = s0 is not None
    padded = T < RET_CHUNK
    assert padded or T % RET_CHUNK == 0
    tok = pl.BlockSpec((1, T, D), lambda b: (b, 0, 0))
    key = tok if padded else pl.BlockSpec((1, D, T), lambda b: (b, 0, 0))
    state = pl.BlockSpec((1, RET_HEADS, RET_DK, RET_DV), lambda b: (b, 0, 0, 0))
    in_specs = [tok, key, tok] + ([state] if has_s0 else [])
    args = (rq, rk, rv) + ((s0,) if has_s0 else ())
    return pl.pallas_call(
        functools.partial(_retention_kernel, T=T, L=L, has_s0=has_s0),
        grid=(B,),
        in_specs=in_specs,
        out_specs=[tok, state],
        out_shape=[jax.ShapeDtypeStruct((B, T, D), out_dtype),
                   jax.ShapeDtypeStruct((B, RET_HEADS, RET_DK, RET_DV), F32)],
        scratch_shapes=[pltpu.VMEM((RET_CHUNK, D), F32)] * 3 if padded else [],
        compiler_params=_params(("parallel",)),
        name="retention",
    )(*args)


def _post_kernel(*refs, apply_uv):
    if apply_uv:
        (x_ref, att_ref, y_ref, ga_ref, gr_ref, mod_ref, wuv_ref, wo_ref, wup_ref, wdn_ref,
         g1_ref, b1_ref, g2_ref, b2_ref, o_ref) = refs
    else:
        (x_ref, att_ref, y_ref, ga_ref, gr_ref, mod_ref, wo_ref, wup_ref, wdn_ref,
         g1_ref, b1_ref, g2_ref, b2_ref, o_ref) = refs
    D = D_MODEL
    tm = x_ref.shape[1]
    n_sub = POST_SPLIT if tm % (POST_SPLIT * 16) == 0 else 1
    subs = [slice(i * (tm // n_sub), (i + 1) * (tm // n_sub)) for i in range(n_sub)]

    def mod(rows, k):
        cols = slice(k * D, (k + 1) * D)
        return mod_ref[0, :, cols] if mod_ref.shape[1] == 1 else mod_ref[0, rows, cols]

    def mix_in(rows):
        if apply_uv:
            o_a = jnp.concatenate([_dot(att_ref[0, head, rows, :].astype(BF16), wuv_ref[head])
                                   for head in range(MLA_HEADS)], axis=1).astype(BF16)
        else:
            o_a = att_ref[0, rows, :]
        y = y_ref[0, rows, :].astype(F32)
        normed = []
        for head in range(RET_HEADS):
            yh = y[:, head * RET_DV:(head + 1) * RET_DV]
            yc = yh - jnp.mean(yh, axis=-1, keepdims=True)
            normed.append((yc * lax.rsqrt(jnp.mean(yc * yc, axis=-1, keepdims=True) + LN_EPS)).astype(BF16))
        return ga_ref[0, rows, :] * o_a + gr_ref[0, rows, :] * jnp.concatenate(normed, axis=1)

    mixv = [mix_in(rows) for rows in subs]
    mix = [_dot(mv, wo_ref[...]) for mv in mixv]
    x1 = [_layer_norm(ALPHA * x_ref[0, rows, :] + (1.0 + mod(rows, 2)) * mx, g1_ref[...], b1_ref[...])
          for rows, mx in zip(subs, mix)]
    h2 = [(x * (1.0 + mod(rows, 4)) + mod(rows, 3)).astype(BF16) for rows, x in zip(subs, x1)]

    def up(unit):
        i, c = unit
        return _dot(h2[i], wup_ref[:, c * D:(c + 1) * D])

    units = [(i, c) for i in range(n_sub) for c in range(D_FF // D)]
    m = [None] * n_sub
    u_next = up(units[0])
    for idx, (i, c) in enumerate(units):
        u = jnp.maximum(u_next, 0.0)
        if idx + 1 < len(units):
            u_next = up(units[idx + 1])
        part = _dot((u * u).astype(BF16), wdn_ref[c * D:(c + 1) * D, :])
        m[i] = part if m[i] is None else m[i] + part
    for i, rows in enumerate(subs):
        o_ref[0, rows, :] = _layer_norm(ALPHA * x1[i] + (1.0 + mod(rows, 5)) * m[i], g2_ref[...], b2_ref[...])


def _post(x3, att, y3, ga, gr, mod3, w, *, tm, apply_uv):
    G, Sg, D = x3.shape
    R = mod3.shape[1]
    row = lambda g, i: (g, i, 0)
    act = pl.BlockSpec((1, tm, D), row)
    if apply_uv:
        att_spec = pl.BlockSpec((1, MLA_HEADS, tm, KV_LORA), lambda g, i: (g, 0, i, 0))
    else:
        att_spec = act
    in_specs = [act, att_spec, act, act, act, pl.BlockSpec((1, R, 6 * D), lambda g, i: (g, 0, 0))]
    args = [x3, att, y3, ga, gr, mod3]
    names = (["w_uv"] if apply_uv else []) + ["w_o", "w_up", "w_down", "ln1_g", "ln1_b", "ln2_g", "ln2_b"]
    for n in names:
        in_specs.append(_resident(w[n].shape))
        args.append(w[n])
    return pl.pallas_call(
        functools.partial(_post_kernel, apply_uv=apply_uv),
        grid=(G, Sg // tm),
        in_specs=in_specs,
        out_specs=act,
        out_shape=jax.ShapeDtypeStruct((G, Sg, D), F32),
        compiler_params=_params(("parallel", "parallel")),
        name="post",
    )(*args)


def _rope_tables(pos, dim):
    half = dim // 2
    inv = ROPE_BASE ** (-jnp.arange(half, dtype=F32) / half)
    ang = pos.astype(F32)[:, None] * inv[None, :]
    cos, sin = jnp.cos(ang), jnp.sin(ang)
    reps = LANES // dim
    return (jnp.tile(jnp.concatenate([cos, cos], -1), (1, reps)),
            jnp.tile(jnp.concatenate([-sin, sin], -1), (1, reps)))


def kernel(x_prompt, x_sample, c_prompt, c_sample, cache_ckv, cache_kpe, state_ret, page_table,
           w_ada, b_ada, w_in, g_qnorm, g_kvnorm, w_uq, w_uk, w_uv, w_o,
           ln1_g, ln1_b, w_up, w_down, ln2_g, ln2_b):
    B, S, D = x_prompt.shape
    DB, T, _ = x_sample.shape
    n_pages = page_table.shape[1]
    past_len = n_pages * PAGE_SIZE
    assert DEPTH == 1 and w_in.shape[0] == 1
    l = 0
    n_head = Q_LORA + KV_LORA + QK_ROPE

    w_in_t = jnp.swapaxes(w_in[l], 0, 1)
    w = {
        "w_head": jnp.concatenate([w_in_t[:n_head], w_in_t[Q_LORA + KV_LORA:n_head]], axis=0).astype(BF16),
        "w_r": w_in_t[n_head:].astype(BF16),
        "g_q": g_qnorm[l][None, :], "g_kv": g_kvnorm[l][None, :],
        "w_uq_nope": w_uq[l][:, :, :QK_NOPE].reshape(Q_LORA, MLA_HEADS * QK_NOPE).astype(BF16),
        "w_uq_pe": w_uq[l][:, :, QK_NOPE:].reshape(Q_LORA, MLA_HEADS * QK_ROPE).astype(BF16),
        "w_ukT": jnp.transpose(w_uk[l], (1, 2, 0)).astype(BF16),
        "w_uk_flat": w_uk[l].reshape(KV_LORA, MLA_HEADS * QK_NOPE).astype(BF16),
        "w_uv": jnp.transpose(w_uv[l], (1, 0, 2)).astype(BF16),
        "w_o": w_o[l].astype(BF16), "w_up": w_up[l].astype(BF16), "w_down": w_down[l].astype(BF16),
        "ln1_g": ln1_g[l][None, :], "ln1_b": ln1_b[l][None, :],
        "ln2_g": ln2_g[l][None, :], "ln2_b": ln2_b[l][None, :],
    }

    n_c = B + DB
    pad = (-n_c) % 16
    c_all = jnp.concatenate([c_prompt, c_sample, jnp.zeros((pad, D), F32)], axis=0)
    ada = _ada(c_all, w_ada[l], b_ada[l][None, :])
    mod_p = ada[:B].reshape(B, 1, 6 * D)
    mod_s = jnp.repeat(ada[B:n_c], T, axis=0).reshape(1, DB * T, 6 * D)

    pos_p = jnp.arange(S)
    tabs_p = _rope_tables(pos_p, QK_ROPE) + _rope_tables(pos_p, RET_DK)
    (ckv_p, kpe_p, kc_p, qf_p, kf_p, rq_p, rk_p, rv_p, ga_p, gr_p) = _inproj(
        x_prompt, mod_p, tabs_p, w, tm=512, act_dtype=BF16, absorb=False)
    kpe_p = jnp.swapaxes(kpe_p, 1, 2)
    oa_p = _mla_prompt(qf_p, kf_p, kc_p, w["w_uv"], tq=256)
    y_p, ret_p = _retention(rq_p, rk_p, rv_p, None, L=min(RET_CHUNK, S), out_dtype=BF16)
    out_p = _post(x_prompt, oa_p, y_p, ga_p, gr_p, mod_p, w, tm=512, apply_uv=False)

    n_s = DB * T
    pos_s = jnp.tile(past_len + jnp.arange(T), DB)
    tabs_s = _rope_tables(pos_s, QK_ROPE) + _rope_tables(pos_s, RET_DK)
    xs = x_sample.reshape(1, n_s, D)
    (ckv_s, kpe_s, _, _, qlat_s, qpe_s, rq_s, rk_s, rv_s, ga_s, gr_s) = _inproj(
        xs, mod_s, tabs_s, w, tm=n_s, act_dtype=F32, absorb=True)
    kpe_s = jnp.swapaxes(kpe_s, 1, 2)

    def to_batch_rows(q):
        d = q.shape[-1]
        return q[0].reshape(MLA_HEADS, DB, T, d).transpose(1, 0, 2, 3).reshape(DB, MLA_HEADS * T, d)

    olat_s = _mla_sample(page_table, to_batch_rows(qlat_s), to_batch_rows(qpe_s),
                         ckv_s.reshape(DB, T, KV_LORA), kpe_s.reshape(DB, T, QK_ROPE),
                         cache_ckv, jnp.swapaxes(cache_kpe, 2, 3), pages=16)
    olat_s = olat_s.reshape(DB, MLA_HEADS, T, KV_LORA).transpose(1, 0, 2, 3).reshape(1, MLA_HEADS, n_s, KV_LORA)
    y_s, ret_s = _retention(rq_s.reshape(DB, T, D), rk_s.reshape(DB, T, D), rv_s.reshape(DB, T, D),
                            state_ret[l], L=T, out_dtype=F32)
    out_s = _post(xs, olat_s, y_s.reshape(1, n_s, D), ga_s, gr_s, mod_s, w, tm=n_s, apply_uv=True)

    return (out_p, out_s.reshape(DB, T, D),
            ckv_p[None], kpe_p[None], ret_p[None],
            ckv_s.reshape(1, DB, T, KV_LORA), kpe_s.reshape(1, DB, T, QK_ROPE), ret_s[None])
```

```python
import functools
import math

import jax
import jax.numpy as jnp
from jax import lax
from jax.experimental import pallas as pl
from jax.experimental.pallas import tpu as pltpu

D_MODEL = 1024
DEPTH = 1
PAGE_SIZE = 128
MLA_HEADS = 8
Q_LORA = 384
KV_LORA = 256
QK_NOPE = 128
QK_ROPE = 64
V_HEAD = D_MODEL // MLA_HEADS
MLA_SCALE = (QK_NOPE + QK_ROPE) ** -0.5
LOG2E = 1.4426950408889634
Q_SCALE = MLA_SCALE * LOG2E
RET_HEADS = 8
RET_DK = 128
RET_DV = D_MODEL // RET_HEADS
RET_CHUNK = 128
D_FF = 4 * D_MODEL
ROPE_BASE = 10000.0
LN_EPS = 1e-5
RMS_EPS = 1e-6
NEG = -1e30
ALPHA = (2.0 * DEPTH) ** 0.25

PROMPT_ROW_TILE = 512
ATTN_TILE = 256
SAMPLE_PAGE_GROUP = 16
SCORE_LOOKAHEAD = 3
POST_SPLIT = 2
RET_CHUNKS_PER_TRIP = 2
RET_ROWS_PER_STEP = 8
NEXT_ROW_BURST = 2
LANES = 128
V7X_VMEM_BYTES = 64 * 1024 * 1024
VMEM_LIMIT = V7X_VMEM_BYTES - 8 * 1024 * 1024

F32 = jnp.float32
BF16 = jnp.bfloat16


def _params(sem):
    return pltpu.CompilerParams(dimension_semantics=sem, vmem_limit_bytes=VMEM_LIMIT)


def _resident(shape):
    nd = len(shape)
    return pl.BlockSpec(shape, lambda *_: (0,) * nd, pipeline_mode=pl.Buffered(1))


def _dot(a, b):
    return jnp.dot(a, b, preferred_element_type=F32)


def _dot_nt(a, b):
    return lax.dot_general(a, b, (((1,), (1,)), ((), ())), preferred_element_type=F32)


def _rope(x, cos, sin_signed, half):
    lanes = x.shape[-1]
    if 2 * half == lanes:
        partner = pltpu.roll(x, half, axis=1)
    else:
        lane = lax.broadcasted_iota(jnp.int32, x.shape, 1)
        first = (lane & (2 * half - 1)) < half
        partner = jnp.where(first, pltpu.roll(x, lanes - half, axis=1), pltpu.roll(x, half, axis=1))
    return x * cos + partner * sin_signed


def _layer_norm(v, g, b):
    mu = jnp.mean(v, axis=-1, keepdims=True)
    vc = v - mu
    var = jnp.mean(vc * vc, axis=-1, keepdims=True)
    return vc * lax.rsqrt(var + LN_EPS) * g + b


def _ada_kernel(c_ref, w_ref, b_ref, o_ref):
    c = c_ref[...]
    s = (c * jax.nn.sigmoid(c)).astype(BF16)
    o_ref[...] = _dot(s, w_ref[...].astype(BF16)) + b_ref[...]


def _ada(c_all, w_ada, b_ada):
    rows = c_all.shape[0]
    tn = D_MODEL
    return pl.pallas_call(
        _ada_kernel,
        grid=(6 * D_MODEL // tn,),
        in_specs=[pl.BlockSpec((rows, D_MODEL), lambda j: (0, 0)),
                  pl.BlockSpec((D_MODEL, tn), lambda j: (0, j)),
                  pl.BlockSpec((1, tn), lambda j: (0, j))],
        out_specs=pl.BlockSpec((rows, tn), lambda j: (0, j)),
        out_shape=jax.ShapeDtypeStruct((rows, 6 * D_MODEL), F32),
        compiler_params=_params(("parallel",)),
        name="ada",
    )(c_all, w_ada, b_ada)


def _inproj_kernel(x_ref, mod_ref, cosa_ref, sina_ref, cosb_ref, sinb_ref,
                   wh_ref, wr_ref, gq_ref, gkv_ref, wqn_ref, wqp_ref, wuk_ref,
                   ckv_ref, kpe_ref, kc_ref, *refs, absorb):
    if absorb:
        kp_ref, qlat_ref, qpe_ref, rq_ref, rk_ref, rv_ref, ga_ref, gr_ref = refs
    else:
        qf_ref, kf_ref, rq_ref, rk_ref, rv_ref, ga_ref, gr_ref = refs
    D = D_MODEL
    x = x_ref[0]
    sh1 = mod_ref[0, :, 0:D]
    sc1 = mod_ref[0, :, D:2 * D]
    h = (x * (1.0 + sc1) + sh1).astype(BF16)

    yh = _dot_nt(h, wh_ref[...])
    pq = yh[:, 0:Q_LORA]
    pkv = yh[:, Q_LORA:Q_LORA + KV_LORA]
    pkr = yh[:, Q_LORA + KV_LORA:Q_LORA + KV_LORA + LANES]
    qn = (pq * lax.rsqrt(jnp.mean(pq * pq, axis=-1, keepdims=True) + RMS_EPS) * gq_ref[...]).astype(BF16)
    ckv = pkv * lax.rsqrt(jnp.mean(pkv * pkv, axis=-1, keepdims=True) + RMS_EPS) * gkv_ref[...]
    ckv_ref[0] = ckv
    kc_ref[0] = ckv.astype(BF16)
    cosa = cosa_ref[...]
    sina = sina_ref[...]
    kpe2 = _rope(pkr, cosa, sina, QK_ROPE // 2)
    kpe_ref[0] = kpe2.T[0:QK_ROPE, :]

    qpe = _dot(qn, wqp_ref[...])
    qnope = _dot(qn, wqn_ref[...])
    pe_blocks = [_rope(qpe[:, j * LANES:(j + 1) * LANES], cosa, sina, QK_ROPE // 2) * Q_SCALE
                 for j in range(MLA_HEADS * QK_ROPE // LANES)]
    if absorb:
        kp_ref[0] = kpe2[:, 0:QK_ROPE].astype(BF16)
        for head in range(MLA_HEADS):
            off = (head % 2) * QK_ROPE
            qpe_ref[0, head] = pe_blocks[head // 2][:, off:off + QK_ROPE].astype(BF16)
            qh = qnope[:, head * QK_NOPE:(head + 1) * QK_NOPE].astype(BF16)
            qlat_ref[0, head] = (_dot(qh, wuk_ref[head]) * Q_SCALE).astype(BF16)
    else:
        lane = lax.broadcasted_iota(jnp.int32, kpe2.shape, 1)
        kpe_pad = jnp.where(lane < QK_ROPE, kpe2, 0.0).astype(BF16)
        knope = _dot(ckv.astype(BF16), wuk_ref[...])
        for head in range(MLA_HEADS):
            sl = slice(head * QK_NOPE, (head + 1) * QK_NOPE)
            kf_ref[0, head, :, 0:QK_NOPE] = knope[:, sl].astype(BF16)
            kf_ref[0, head, :, QK_NOPE:2 * QK_NOPE] = kpe_pad
            qf_ref[0, head, :, 0:QK_NOPE] = (qnope[:, sl] * Q_SCALE).astype(BF16)
            blk = pe_blocks[head // 2]
            if head % 2:
                blk = pltpu.roll(blk, QK_ROPE, axis=1)
            qf_ref[0, head, :, QK_NOPE:2 * QK_NOPE] = blk.astype(BF16)

    cosb = cosb_ref[...]
    sinb = sinb_ref[...]
    rq = _dot_nt(h, wr_ref[0:D, :])
    rk = _dot_nt(h, wr_ref[D:2 * D, :])
    for head in range(RET_HEADS):
        sl = slice(head * RET_DK, (head + 1) * RET_DK)
        rq_ref[0, :, sl] = _rope(rq[:, sl], cosb, sinb, RET_DK // 2).astype(rq_ref.dtype)
        rk_h = _rope(rk[:, sl], cosb, sinb, RET_DK // 2) * (RET_DK ** -0.5)
        if absorb:
            rk_ref[0, :, sl] = rk_h.astype(rk_ref.dtype)
        else:
            rk_ref[0, sl, :] = rk_h.T.astype(rk_ref.dtype)
    rv_ref[0] = _dot_nt(h, wr_ref[2 * D:3 * D, :]).astype(rv_ref.dtype)
    rg = _dot_nt(h, wr_ref[3 * D:4 * D, :])
    ga_ref[0] = jax.nn.sigmoid(_dot_nt(h, wr_ref[4 * D:5 * D, :])).astype(BF16)
    gr = jax.nn.sigmoid(_dot_nt(h, wr_ref[5 * D:6 * D, :]))
    gr_ref[0] = (gr * (rg * jax.nn.sigmoid(rg))).astype(BF16)


def _inproj(x3, mod3, tabs, w, *, tm, act_dtype, absorb):
    G, Sg, D = x3.shape
    R = mod3.shape[1]
    assert Sg % tm == 0 and R in (1, tm) and (R == 1 or Sg == tm)
    row = lambda g, i: (g, i, 0)
    head_row = lambda g, i: (g, 0, i, 0)
    tab = pl.BlockSpec((tm, LANES), lambda g, i: (i, 0))
    w_uk = w["w_ukT"] if absorb else w["w_uk_flat"]
    in_specs = [
        pl.BlockSpec((1, tm, D), row),
        pl.BlockSpec((1, R, 2 * D), lambda g, i: (g, 0, 0)),
        tab, tab, tab, tab,
        _resident(w["w_head"].shape), _resident(w["w_r"].shape),
        _resident(w["g_q"].shape), _resident(w["g_kv"].shape),
        _resident(w["w_uq_nope"].shape), _resident(w["w_uq_pe"].shape), _resident(w_uk.shape),
    ]

    def tok(width, dtype):
        return jax.ShapeDtypeStruct((G, Sg, width), dtype), pl.BlockSpec((1, tm, width), row)

    def per_head(width):
        return (jax.ShapeDtypeStruct((G, MLA_HEADS, Sg, width), BF16),
                pl.BlockSpec((1, MLA_HEADS, tm, width), head_row))

    kpe_t = (jax.ShapeDtypeStruct((G, QK_ROPE, Sg), F32), pl.BlockSpec((1, QK_ROPE, tm), lambda g, i: (g, 0, i)))
    outs = [tok(KV_LORA, F32), kpe_t, tok(KV_LORA, BF16)]
    if absorb:
        outs += [tok(QK_ROPE, BF16), per_head(KV_LORA), per_head(QK_ROPE)]
    else:
        outs += [per_head(2 * QK_NOPE), per_head(2 * QK_NOPE)]
    ret_k = tok(D, act_dtype) if absorb else (jax.ShapeDtypeStruct((G, D, Sg), act_dtype),
                                              pl.BlockSpec((1, D, tm), lambda g, i: (g, 0, i)))
    outs += [tok(D, act_dtype), ret_k, tok(D, act_dtype), tok(D, BF16), tok(D, BF16)]
    return pl.pallas_call(
        functools.partial(_inproj_kernel, absorb=absorb),
        grid=(G, Sg // tm),
        in_specs=in_specs, out_specs=[o[1] for o in outs], out_shape=[o[0] for o in outs],
        compiler_params=_params(("parallel", "parallel")),
        name="inproj",
    )(x3, mod3, *tabs, w["w_head"], w["w_r"], w["g_q"], w["g_kv"],
      w["w_uq_nope"], w["w_uq_pe"], w_uk)


def _mla_prompt_kernel(qf_ref, kf_ref, v_ref, wuv_ref, o_ref,
                       m_ref, l_ref, acc_ref, s_ref, *, tq):
    qi = pl.program_id(1)

    def rows_of(j):
        return pl.ds(pl.multiple_of(j * tq, tq), tq)

    def scores(head, j):
        return _dot_nt(qf_ref[0, head], kf_ref[0, head, rows_of(j), :])

    for hd in range(SCORE_LOOKAHEAD):
        s_ref[hd] = scores(hd, 0)

    def tile(j, first, last):
        v = v_ref[0, rows_of(j), :]
        col = lax.broadcasted_iota(jnp.int32, (tq, tq), 1)
        row = lax.broadcasted_iota(jnp.int32, (tq, tq), 0)
        if first:
            keep = col <= row + qi * tq
        elif last:
            keep = col <= row
        pending = [s_ref[hd] for hd in range(SCORE_LOOKAHEAD)]
        for head in range(MLA_HEADS):
            s = pending.pop(0)
            ahead = head + SCORE_LOOKAHEAD
            if ahead < MLA_HEADS:
                pending.append(scores(ahead, j))
            elif not last:
                s_ref[ahead - MLA_HEADS] = scores(ahead - MLA_HEADS, j + 1)
            if first or last:
                s = jnp.where(keep, s, NEG)
            m_cur = jnp.max(s, axis=-1, keepdims=True)
            m_new = jnp.broadcast_to(m_cur, (tq, LANES)) if first else jnp.maximum(m_ref[head], m_cur)
            p = jnp.exp2(s - jnp.concatenate([m_new] * (tq // LANES), axis=1))
            psum = p[:, 0:LANES]
            for c in range(1, tq // LANES):
                psum = psum + p[:, c * LANES:(c + 1) * LANES]
            pv = _dot(p.astype(BF16), v)
            if first:
                l_ref[head] = psum
                acc_ref[head] = pv
            else:
                a = jnp.exp2(m_ref[head] - m_new)
                l_ref[head] = a * l_ref[head] + psum
                acc_ref[head] = jnp.concatenate([a] * (KV_LORA // LANES), axis=1) * acc_ref[head] + pv
            m_ref[head] = m_new

    n_mid = jnp.maximum(qi - 1, 0)

    def pair(i, carry):
        tile(1 + 2 * i, False, False)
        tile(2 + 2 * i, False, False)
        return carry

    tile(0, True, False)
    lax.fori_loop(0, n_mid >> 1, pair, 0)

    @pl.when((n_mid & 1) == 1)
    def _():
        tile(qi - 1, False, False)

    @pl.when(qi > 0)
    def _():
        tile(qi, False, True)

    heads = range(MLA_HEADS)
    inv = [1.0 / jnp.sum(l_ref[head], axis=-1, keepdims=True) for head in heads]
    lat = [(acc_ref[head] * inv[head]).astype(BF16) for head in heads]
    out = [_dot(lat[head], wuv_ref[head]) for head in heads]
    for head in heads:
        o_ref[0, :, head * V_HEAD:(head + 1) * V_HEAD] = out[head].astype(BF16)


def _mla_prompt(qf, kf, v, w_uv, *, tq):
    B, Hh, S, width = qf.shape
    assert tq % LANES == 0 and S >= 2 * tq
    return pl.pallas_call(
        functools.partial(_mla_prompt_kernel, tq=tq),
        grid=(B, S // tq),
        in_specs=[pl.BlockSpec((1, Hh, tq, width), lambda b, i: (b, 0, i, 0)),
                  pl.BlockSpec((1, Hh, S, width), lambda b, i: (b, 0, 0, 0)),
                  pl.BlockSpec((1, S, KV_LORA), lambda b, i: (b, 0, 0)),
                  _resident(w_uv.shape)],
        out_specs=pl.BlockSpec((1, tq, D_MODEL), lambda b, i: (b, i, 0)),
        out_shape=jax.ShapeDtypeStruct((B, S, D_MODEL), BF16),
        scratch_shapes=[pltpu.VMEM((Hh, tq, LANES), F32), pltpu.VMEM((Hh, tq, LANES), F32),
                        pltpu.VMEM((Hh, tq, KV_LORA), F32), pltpu.VMEM((SCORE_LOOKAHEAD, tq, tq), F32)],
        compiler_params=_params(("parallel", "parallel")),
        name="mla_prompt",
    )(qf, kf, v, w_uv)


def _mla_sample_kernel(pt_ref, q1_ref, q2_ref, cn_ref, kn_ref, ck_hbm, kp_hbm, o_ref,
                       ck_buf, kp_buf, sems, *, pages, groups, tokens):
    b = pl.program_id(0)
    n_b = pl.num_programs(0)
    n_pages = pages * groups

    def page_copies(bb, i):
        slot = lax.rem(bb, 2)
        page = pt_ref[bb, i]
        return (pltpu.make_async_copy(ck_hbm.at[0, page], ck_buf.at[slot, i], sems.at[0, slot]),
                pltpu.make_async_copy(kp_hbm.at[0, page], kp_buf.at[slot, i], sems.at[1, slot]))

    def start_page(bb, i):
        for cp in page_copies(bb, i):
            cp.start()

    @pl.when(b == 0)
    def _():
        for i in range(n_pages):
            start_page(b, i)

    for i in range(n_pages):
        for cp in page_copies(b, i):
            cp.wait()

    def start_next_row(i):
        @pl.when(b + 1 < n_b)
        def _():
            start_page(b + 1, i)

    q1 = q1_ref[0]
    q2 = q2_ref[0]
    rows = q1.shape[0]
    slot = lax.rem(b, 2)

    def scores(g):
        cols = []
        for i in range(g * pages, (g + 1) * pages):
            cols.append(_dot_nt(q1, ck_buf[slot, i].astype(BF16)) + _dot(q2, kp_buf[slot, i].astype(BF16)))
            if i < n_pages // NEXT_ROW_BURST:
                for k in range(NEXT_ROW_BURST):
                    start_next_row(NEXT_ROW_BURST * i + k)
        return cols

    def update(g, cols, m, l, acc):
        mx = cols[0]
        for c in cols[1:]:
            mx = jnp.maximum(mx, c)
        m_new = jnp.maximum(m, jnp.max(mx, axis=-1, keepdims=True))
        a = jnp.exp2(m - m_new)
        ps = [jnp.exp2(c - m_new) for c in cols]
        psum = ps[0]
        for p in ps[1:]:
            psum = psum + p
        pv = None
        for j in range(pages):
            part = _dot(ps[j].astype(BF16), ck_buf[slot, g * pages + j].astype(BF16))
            pv = part if pv is None else pv + part
        acc = jnp.concatenate([a] * (KV_LORA // LANES), axis=1) * acc + pv
        return m_new, a * l + psum, acc

    m = jnp.full((rows, LANES), -jnp.inf, F32)
    l = jnp.zeros((rows, LANES), F32)
    acc = jnp.zeros((rows, KV_LORA), F32)
    pending = scores(0)
    for g in range(groups):
        nxt = scores(g + 1) if g + 1 < groups else None
        m, l, acc = update(g, pending, m, l, acc)
        pending = nxt

    q1f = q1.astype(F32)
    q2f = q2.astype(F32)
    cn = cn_ref[0]
    kn = kn_ref[0]
    tok = lax.broadcasted_iota(jnp.int32, (rows, LANES), 0) & (tokens - 1)
    cols = []
    for t in range(tokens):
        sc = (jnp.sum(q1f * cn[t:t + 1, :], axis=-1, keepdims=True)
              + jnp.sum(q2f * kn[t:t + 1, :], axis=-1, keepdims=True))
        cols.append(jnp.where(tok >= t, sc, NEG))
    m_fin = m
    for sc in cols:
        m_fin = jnp.maximum(m_fin, sc)
    a2 = jnp.exp2(m - m_fin)
    l_fin = jnp.sum(a2 * l, axis=-1, keepdims=True)
    acc = jnp.concatenate([a2] * (KV_LORA // LANES), axis=1) * acc
    for t, sc in enumerate(cols):
        pt = jnp.exp2(sc - m_fin)
        l_fin = l_fin + pt[:, 0:1]
        acc = acc + jnp.concatenate([pt] * (KV_LORA // LANES), axis=1) * cn[t:t + 1, :]
    o_ref[0] = acc * (1.0 / l_fin)


def _mla_sample(page_table, q1, q2, cnew, knew, cache_ckv, cache_kpe_t, *, pages):
    DB, n_pages = page_table.shape
    rows = q1.shape[1]
    tokens = cnew.shape[1]
    assert n_pages % pages == 0 and tokens & (tokens - 1) == 0
    groups = n_pages // pages

    per_batch = lambda b, pt: (b, 0, 0)
    grid_spec = pltpu.PrefetchScalarGridSpec(
        num_scalar_prefetch=1,
        grid=(DB,),
        in_specs=[pl.BlockSpec((1, rows, KV_LORA), per_batch),
                  pl.BlockSpec((1, rows, QK_ROPE), per_batch),
                  pl.BlockSpec((1, tokens, KV_LORA), per_batch),
                  pl.BlockSpec((1, tokens, QK_ROPE), per_batch),
                  pl.BlockSpec(memory_space=pl.ANY),
                  pl.BlockSpec(memory_space=pl.ANY)],
        out_specs=pl.BlockSpec((1, rows, KV_LORA), per_batch),
        scratch_shapes=[pltpu.VMEM((2, n_pages, PAGE_SIZE, KV_LORA), F32),
                        pltpu.VMEM((2, n_pages, QK_ROPE, PAGE_SIZE), F32),
                        pltpu.SemaphoreType.DMA((2, 2))],
    )
    return pl.pallas_call(
        functools.partial(_mla_sample_kernel, pages=pages, groups=groups, tokens=tokens),
        grid_spec=grid_spec,
        out_shape=jax.ShapeDtypeStruct((DB, rows, KV_LORA), F32),
        compiler_params=_params(("arbitrary",)),
        name="mla_sample",
    )(page_table, q1, q2, cnew, knew, cache_ckv, cache_kpe_t)


def _retention_kernel(*refs, T, L, has_s0):
    C = RET_CHUNK
    padded = T < C
    if has_s0:
        q_ref, k_ref, v_ref, s0_ref, y_ref, s_ref = refs[:6]
        rest = refs[6:]
    else:
        q_ref, k_ref, v_ref, y_ref, s_ref = refs[:5]
        rest = refs[5:]
    heads = range(RET_HEADS)
    ri = lax.broadcasted_iota(jnp.int32, (C, C), 0).astype(F32)
    ci = lax.broadcasted_iota(jnp.int32, (C, C), 1).astype(F32)
    diff = ri - ci
    causal = diff >= 0.0
    log_gamma = [math.log(1.0 - 2.0 ** (-5.0 - h)) for h in heads]
    dmask = [jnp.where(causal, jnp.exp(jnp.where(causal, diff, 0.0) * lg), 0.0) for lg in log_gamma]
    q_decay = [jnp.exp((ri + 1.0) * lg) for lg in log_gamma]
    k_decay = [jnp.exp((L - 1.0 - ci) * lg) for lg in log_gamma]
    chunk_decay = [math.exp(L * lg) for lg in log_gamma]

    def col(h):
        return slice(h * RET_DK, (h + 1) * RET_DK)

    def chunks(streams):
        pre = []
        for r, items in streams:
            scores = [[_dot(q[h], kt[h].astype(BF16)) for h in heads] for q, kt, v, _ in items]
            kd_t = [[(kt[h].astype(F32) * k_decay[h]).astype(BF16) for h in heads] for q, kt, v, _ in items]
            pre.append((scores, kd_t))
        inner = [[[_dot((sc[h] * dmask[h]).astype(BF16), v[h]) for h in heads]
                  for sc, (q, kt, v, _) in zip(scores, items)]
                 for (scores, _), (r, items) in zip(pre, streams)]
        for (r, items), (_, kd_t), inn_r in zip(streams, pre, inner):
            s = [s_ref[r, h] for h in heads]
            for (q, kt, v, store), inn, kd in zip(items, inn_r, kd_t):
                cross = [_dot(q[h], s[h].astype(BF16)) for h in heads]
                upd = [_dot(kd[h], v[h]) for h in heads]
                for h in heads:
                    store(h, inn[h] + cross[h] * q_decay[h])
                s = [chunk_decay[h] * s[h] + upd[h] for h in heads]
            for h in heads:
                s_ref[r, h] = s[h]

    n_rows = q_ref.shape[0]
    for r in range(n_rows):
        for h in heads:
            s_ref[r, h] = s0_ref[r, h] if has_s0 else jnp.zeros((RET_DK, RET_DV), F32)
    if padded:
        qp_ref, kp_ref, vp_ref = rest
        streams = []
        for r in range(n_rows):
            for src, dst in zip((q_ref, k_ref, v_ref), rest):
                dst[r] = jnp.zeros(dst.shape[1:], F32)
                dst[r, 0:T, :] = src[r].astype(F32)

            def store(h, y, r=r):
                y_ref[r, :, col(h)] = y[0:T, :].astype(y_ref.dtype)
            streams.append((r, [([qp_ref[r, :, col(h)].astype(BF16) for h in heads],
                                 [kp_ref[r, :, col(h)].T for h in heads],
                                 [vp_ref[r, :, col(h)].astype(BF16) for h in heads], store)]))
        chunks(streams)
    else:
        per_trip = RET_CHUNKS_PER_TRIP if (T // C) % RET_CHUNKS_PER_TRIP == 0 else 1

        def item(c):
            rows = pl.ds(pl.multiple_of(c * C, C), C)

            def store(h, y):
                y_ref[0, rows, col(h)] = y.astype(y_ref.dtype)
            return ([q_ref[0, rows, col(h)] for h in heads],
                    [k_ref[0, col(h), rows] for h in heads],
                    [v_ref[0, rows, col(h)] for h in heads], store)

        def body(i, carry):
            chunks([(0, [item(i * per_trip + k) for k in range(per_trip)])])
            return carry
        lax.fori_loop(0, T // C // per_trip, body, 0)


def _retention(rq, rk, rv, s0, *, L, out_dtype):
    B, T, D = rq.shape
    has_s0 = s0 is not None
    padded = T < RET_CHUNK
    assert padded or T % RET_CHUNK == 0
    rb = RET_ROWS_PER_STEP if padded and B % RET_ROWS_PER_STEP == 0 else 1
    tok = pl.BlockSpec((rb, T, D), lambda b: (b, 0, 0))
    key = tok if padded else pl.BlockSpec((rb, D, T), lambda b: (b, 0, 0))
    state = pl.BlockSpec((rb, RET_HEADS, RET_DK, RET_DV), lambda b: (b, 0, 0, 0))
    in_specs = [tok, key, tok] + ([state] if has_s0 else [])
    args = (rq, rk, rv) + ((s0,) if has_s0 else ())
    return pl.pallas_call(
        functools.partial(_retention_kernel, T=T, L=L, has_s0=has_s0),
        grid=(B // rb,),
        in_specs=in_specs,
        out_specs=[tok, state],
        out_shape=[jax.ShapeDtypeStruct((B, T, D), out_dtype),
                   jax.ShapeDtypeStruct((B, RET_HEADS, RET_DK, RET_DV), F32)],
        scratch_shapes=[pltpu.VMEM((rb, RET_CHUNK, D), F32)] * 3 if padded else [],
        compiler_params=_params(("parallel",)),
        name="retention",
    )(*args)


def _post_kernel(*refs, apply_uv):
    if apply_uv:
        (x_ref, att_ref, y_ref, ga_ref, gr_ref, mod_ref, wuv_ref, wo_ref, wup_ref, wdn_ref,
         g1_ref, b1_ref, g2_ref, b2_ref, o_ref) = refs
    else:
        (x_ref, att_ref, y_ref, ga_ref, gr_ref, mod_ref, wo_ref, wup_ref, wdn_ref,
         g1_ref, b1_ref, g2_ref, b2_ref, o_ref) = refs
    D = D_MODEL
    tm = x_ref.shape[1]
    n_sub = POST_SPLIT if tm % (POST_SPLIT * 16) == 0 else 1
    subs = [slice(i * (tm // n_sub), (i + 1) * (tm // n_sub)) for i in range(n_sub)]

    def mod(rows, k):
        cols = slice(k * D, (k + 1) * D)
        return mod_ref[0, :, cols] if mod_ref.shape[1] == 1 else mod_ref[0, rows, cols]

    def mix_in(rows):
        if apply_uv:
            o_a = jnp.concatenate([_dot(att_ref[0, head, rows, :].astype(BF16), wuv_ref[head])
                                   for head in range(MLA_HEADS)], axis=1).astype(BF16)
        else:
            o_a = att_ref[0, rows, :]
        y = y_ref[0, rows, :].astype(F32)
        normed = []
        for head in range(RET_HEADS):
            yh = y[:, head * RET_DV:(head + 1) * RET_DV]
            yc = yh - jnp.mean(yh, axis=-1, keepdims=True)
            normed.append((yc * lax.rsqrt(jnp.mean(yc * yc, axis=-1, keepdims=True) + LN_EPS)).astype(BF16))
        return ga_ref[0, rows, :] * o_a + gr_ref[0, rows, :] * jnp.concatenate(normed, axis=1)

    mixv = [mix_in(rows) for rows in subs]
    mix = [_dot(mv, wo_ref[...]) for mv in mixv]
    x1 = [_layer_norm(ALPHA * x_ref[0, rows, :] + (1.0 + mod(rows, 2)) * mx, g1_ref[...], b1_ref[...])
          for rows, mx in zip(subs, mix)]
    h2 = [(x * (1.0 + mod(rows, 4)) + mod(rows, 3)).astype(BF16) for rows, x in zip(subs, x1)]

    def up(unit):
        i, c = unit
        return _dot(h2[i], wup_ref[:, c * D:(c + 1) * D])

    units = [(i, c) for i in range(n_sub) for c in range(D_FF // D)]
    m = [None] * n_sub
    u_next = up(units[0])
    for idx, (i, c) in enumerate(units):
        u = jnp.maximum(u_next, 0.0)
        if idx + 1 < len(units):
            u_next = up(units[idx + 1])
        part = _dot((u * u).astype(BF16), wdn_ref[c * D:(c + 1) * D, :])
        m[i] = part if m[i] is None else m[i] + part
    for i, rows in enumerate(subs):
        o_ref[0, rows, :] = _layer_norm(ALPHA * x1[i] + (1.0 + mod(rows, 5)) * m[i], g2_ref[...], b2_ref[...])


def _post(x3, att, y3, ga, gr, mod3, w, *, tm, apply_uv):
    G, Sg, D = x3.shape
    R = mod3.shape[1]
    row = lambda g, i: (g, i, 0)
    act = pl.BlockSpec((1, tm, D), row)
    if apply_uv:
        att_spec = pl.BlockSpec((1, MLA_HEADS, tm, KV_LORA), lambda g, i: (g, 0, i, 0))
    else:
        att_spec = act
    in_specs = [act, att_spec, act, act, act, pl.BlockSpec((1, R, 6 * D), lambda g, i: (g, 0, 0))]
    args = [x3, att, y3, ga, gr, mod3]
    names = (["w_uv"] if apply_uv else []) + ["w_o", "w_up", "w_down", "ln1_g", "ln1_b", "ln2_g", "ln2_b"]
    for n in names:
        in_specs.append(_resident(w[n].shape))
        args.append(w[n])
    return pl.pallas_call(
        functools.partial(_post_kernel, apply_uv=apply_uv),
        grid=(G, Sg // tm),
        in_specs=in_specs,
        out_specs=act,
        out_shape=jax.ShapeDtypeStruct((G, Sg, D), F32),
        compiler_params=_params(("parallel", "parallel")),
        name="post",
    )(*args)


def _rope_tables(pos, dim):
    half = dim // 2
    inv = ROPE_BASE ** (-jnp.arange(half, dtype=F32) / half)
    ang = pos.astype(F32)[:, None] * inv[None, :]
    cos, sin = jnp.cos(ang), jnp.sin(ang)
    reps = LANES // dim
    return (jnp.tile(jnp.concatenate([cos, cos], -1), (1, reps)),
            jnp.tile(jnp.concatenate([-sin, sin], -1), (1, reps)))


def kernel(x_prompt, x_sample, c_prompt, c_sample, cache_ckv, cache_kpe, state_ret, page_table,
           w_ada, b_ada, w_in, g_qnorm, g_kvnorm, w_uq, w_uk, w_uv, w_o,
           ln1_g, ln1_b, w_up, w_down, ln2_g, ln2_b):
    B, S, D = x_prompt.shape
    DB, T, _ = x_sample.shape
    n_pages = page_table.shape[1]
    past_len = n_pages * PAGE_SIZE
    assert DEPTH == 1 and w_in.shape[0] == 1
    l = 0
    n_head = Q_LORA + KV_LORA + QK_ROPE

    w_in_t = jnp.swapaxes(w_in[l], 0, 1)
    w = {
        "w_head": jnp.concatenate([w_in_t[:n_head], w_in_t[Q_LORA + KV_LORA:n_head]], axis=0).astype(BF16),
        "w_r": w_in_t[n_head:].astype(BF16),
        "g_q": g_qnorm[l][None, :], "g_kv": g_kvnorm[l][None, :],
        "w_uq_nope": w_uq[l][:, :, :QK_NOPE].reshape(Q_LORA, MLA_HEADS * QK_NOPE).astype(BF16),
        "w_uq_pe": w_uq[l][:, :, QK_NOPE:].reshape(Q_LORA, MLA_HEADS * QK_ROPE).astype(BF16),
        "w_ukT": jnp.transpose(w_uk[l], (1, 2, 0)).astype(BF16),
        "w_uk_flat": w_uk[l].reshape(KV_LORA, MLA_HEADS * QK_NOPE).astype(BF16),
        "w_uv": jnp.transpose(w_uv[l], (1, 0, 2)).astype(BF16),
        "w_o": w_o[l].astype(BF16), "w_up": w_up[l].astype(BF16), "w_down": w_down[l].astype(BF16),
        "ln1_g": ln1_g[l][None, :], "ln1_b": ln1_b[l][None, :],
        "ln2_g": ln2_g[l][None, :], "ln2_b": ln2_b[l][None, :],
    }

    n_c = B + DB
    pad = (-n_c) % 16
    c_all = jnp.concatenate([c_prompt, c_sample, jnp.zeros((pad, D), F32)], axis=0)
    ada = _ada(c_all, w_ada[l], b_ada[l][None, :])
    mod_p = ada[:B].reshape(B, 1, 6 * D)
    mod_s = jnp.repeat(ada[B:n_c], T, axis=0).reshape(1, DB * T, 6 * D)

    pos_p = jnp.arange(S)
    tabs_p = _rope_tables(pos_p, QK_ROPE) + _rope_tables(pos_p, RET_DK)
    (ckv_p, kpe_p, kc_p, qf_p, kf_p, rq_p, rk_p, rv_p, ga_p, gr_p) = _inproj(
        x_prompt, mod_p, tabs_p, w, tm=PROMPT_ROW_TILE, act_dtype=BF16, absorb=False)
    kpe_p = jnp.swapaxes(kpe_p, 1, 2)
    oa_p = _mla_prompt(qf_p, kf_p, kc_p, w["w_uv"], tq=ATTN_TILE)
    y_p, ret_p = _retention(rq_p, rk_p, rv_p, None, L=min(RET_CHUNK, S), out_dtype=BF16)
    out_p = _post(x_prompt, oa_p, y_p, ga_p, gr_p, mod_p, w, tm=PROMPT_ROW_TILE, apply_uv=False)

    n_s = DB * T
    pos_s = jnp.tile(past_len + jnp.arange(T), DB)
    tabs_s = _rope_tables(pos_s, QK_ROPE) + _rope_tables(pos_s, RET_DK)
    xs = x_sample.reshape(1, n_s, D)
    (ckv_s, kpe_s, _, _, qlat_s, qpe_s, rq_s, rk_s, rv_s, ga_s, gr_s) = _inproj(
        xs, mod_s, tabs_s, w, tm=n_s, act_dtype=F32, absorb=True)
    kpe_s = jnp.swapaxes(kpe_s, 1, 2)

    def to_batch_rows(q):
        d = q.shape[-1]
        return q[0].reshape(MLA_HEADS, DB, T, d).transpose(1, 0, 2, 3).reshape(DB, MLA_HEADS * T, d)

    olat_s = _mla_sample(page_table, to_batch_rows(qlat_s), to_batch_rows(qpe_s),
                         ckv_s.reshape(DB, T, KV_LORA), kpe_s.reshape(DB, T, QK_ROPE),
                         cache_ckv, jnp.swapaxes(cache_kpe, 2, 3), pages=SAMPLE_PAGE_GROUP)
    olat_s = olat_s.reshape(DB, MLA_HEADS, T, KV_LORA).transpose(1, 0, 2, 3).reshape(1, MLA_HEADS, n_s, KV_LORA)
    y_s, ret_s = _retention(rq_s.reshape(DB, T, D), rk_s.reshape(DB, T, D), rv_s.reshape(DB, T, D),
                            state_ret[l], L=T, out_dtype=F32)
    out_s = _post(xs, olat_s, y_s.reshape(1, n_s, D), ga_s, gr_s, mod_s, w, tm=n_s, apply_uv=True)

    return (out_p, out_s.reshape(DB, T, D),
            ckv_p[None], kpe_p[None], ret_p[None],
            ckv_s.reshape(1, DB, T, KV_LORA), kpe_s.reshape(1, DB, T, QK_ROPE), ret_s[None])
```

```python
import functools
import math

import jax
import jax.numpy as jnp
from jax import lax
from jax.experimental import pallas as pl
from jax.experimental.pallas import tpu as pltpu

D_MODEL = 1024
DEPTH = 1
PAGE_SIZE = 128
MLA_HEADS = 8
Q_LORA = 384
KV_LORA = 256
QK_NOPE = 128
QK_ROPE = 64
V_HEAD = D_MODEL // MLA_HEADS
MLA_SCALE = (QK_NOPE + QK_ROPE) ** -0.5
LOG2E = 1.4426950408889634
Q_SCALE = MLA_SCALE * LOG2E
RET_HEADS = 8
RET_DK = 128
RET_DV = D_MODEL // RET_HEADS
RET_CHUNK = 128
D_FF = 4 * D_MODEL
ROPE_BASE = 10000.0
LN_EPS = 1e-5
RMS_EPS = 1e-6
NEG = -1e30
ALPHA = (2.0 * DEPTH) ** 0.25

ADA_COL_TILE = 2048
PROMPT_ROW_TILE = 512
ATTN_TILE = 256
SAMPLE_PAGE_GROUP = 16
SCORE_LOOKAHEAD = 3
POST_SPLIT = 2
RET_CHUNKS_PER_TRIP = 4
RET_ROWS_PER_STEP = 8
NEXT_ROW_BURST = 2
LANES = 128
V7X_VMEM_BYTES = 64 * 1024 * 1024
VMEM_LIMIT = V7X_VMEM_BYTES - 8 * 1024 * 1024

F32 = jnp.float32
BF16 = jnp.bfloat16


def _params(sem):
    return pltpu.CompilerParams(dimension_semantics=sem, vmem_limit_bytes=VMEM_LIMIT)


def _resident(shape):
    nd = len(shape)
    return pl.BlockSpec(shape, lambda *_: (0,) * nd, pipeline_mode=pl.Buffered(1))


def _dot(a, b):
    return jnp.dot(a, b, preferred_element_type=F32)


def _dot_nt(a, b):
    return lax.dot_general(a, b, (((1,), (1,)), ((), ())), preferred_element_type=F32)


def _rope(x, cos, sin_signed, half):
    lanes = x.shape[-1]
    if 2 * half == lanes:
        partner = pltpu.roll(x, half, axis=1)
    else:
        lane = lax.broadcasted_iota(jnp.int32, x.shape, 1)
        first = (lane & (2 * half - 1)) < half
        partner = jnp.where(first, pltpu.roll(x, lanes - half, axis=1), pltpu.roll(x, half, axis=1))
    return x * cos + partner * sin_signed


def _layer_norm(v, g, b):
    mu = jnp.mean(v, axis=-1, keepdims=True)
    vc = v - mu
    var = jnp.mean(vc * vc, axis=-1, keepdims=True)
    return vc * lax.rsqrt(var + LN_EPS) * g + b


def _ada_kernel(c_ref, w_ref, b_ref, o_ref):
    c = c_ref[...]
    s = (c * jax.nn.sigmoid(c)).astype(BF16)
    o_ref[...] = _dot(s, w_ref[...].astype(BF16)) + b_ref[...]


def _ada(c_all, w_ada, b_ada):
    rows = c_all.shape[0]
    tn = ADA_COL_TILE
    return pl.pallas_call(
        _ada_kernel,
        grid=(6 * D_MODEL // tn,),
        in_specs=[pl.BlockSpec((rows, D_MODEL), lambda j: (0, 0)),
                  pl.BlockSpec((D_MODEL, tn), lambda j: (0, j)),
                  pl.BlockSpec((1, tn), lambda j: (0, j))],
        out_specs=pl.BlockSpec((rows, tn), lambda j: (0, j)),
        out_shape=jax.ShapeDtypeStruct((rows, 6 * D_MODEL), F32),
        compiler_params=_params(("parallel",)),
        name="ada",
    )(c_all, w_ada, b_ada)


def _inproj_kernel(x_ref, mod_ref, cosa_ref, sina_ref, cosb_ref, sinb_ref,
                   wh_ref, wr_ref, gq_ref, gkv_ref, wqn_ref, wqp_ref, wuk_ref,
                   ckv_ref, kpe_ref, kc_ref, *refs, absorb):
    if absorb:
        kp_ref, qlat_ref, qpe_ref, rq_ref, rk_ref, rv_ref, ga_ref, gr_ref = refs
    else:
        qf_ref, kf_ref, rq_ref, rk_ref, rv_ref, ga_ref, gr_ref = refs
    D = D_MODEL
    x = x_ref[0]
    sh1 = mod_ref[0, :, 0:D]
    sc1 = mod_ref[0, :, D:2 * D]
    h = (x * (1.0 + sc1) + sh1).astype(BF16)

    yh = _dot_nt(h, wh_ref[...])
    pq = yh[:, 0:Q_LORA]
    pkv = yh[:, Q_LORA:Q_LORA + KV_LORA]
    pkr = yh[:, Q_LORA + KV_LORA:Q_LORA + KV_LORA + LANES]
    qn = (pq * lax.rsqrt(jnp.mean(pq * pq, axis=-1, keepdims=True) + RMS_EPS) * gq_ref[...]).astype(BF16)
    ckv = pkv * lax.rsqrt(jnp.mean(pkv * pkv, axis=-1, keepdims=True) + RMS_EPS) * gkv_ref[...]
    ckv_ref[0] = ckv
    kc_ref[0] = ckv.astype(BF16)
    cosa = cosa_ref[...]
    sina = sina_ref[...]
    kpe2 = _rope(pkr, cosa, sina, QK_ROPE // 2)
    kpe_ref[0] = kpe2.T[0:QK_ROPE, :]

    qpe = _dot(qn, wqp_ref[...])
    qnope = _dot(qn, wqn_ref[...])
    pe_blocks = [_rope(qpe[:, j * LANES:(j + 1) * LANES], cosa, sina, QK_ROPE // 2) * Q_SCALE
                 for j in range(MLA_HEADS * QK_ROPE // LANES)]
    if absorb:
        kp_ref[0] = kpe2[:, 0:QK_ROPE].astype(BF16)
        for head in range(MLA_HEADS):
            off = (head % 2) * QK_ROPE
            qpe_ref[0, head] = pe_blocks[head // 2][:, off:off + QK_ROPE].astype(BF16)
            qh = qnope[:, head * QK_NOPE:(head + 1) * QK_NOPE].astype(BF16)
            qlat_ref[0, head] = (_dot(qh, wuk_ref[head]) * Q_SCALE).astype(BF16)
    else:
        lane = lax.broadcasted_iota(jnp.int32, kpe2.shape, 1)
        kpe_pad = jnp.where(lane < QK_ROPE, kpe2, 0.0).astype(BF16)
        knope = _dot(ckv.astype(BF16), wuk_ref[...])
        for head in range(MLA_HEADS):
            sl = slice(head * QK_NOPE, (head + 1) * QK_NOPE)
            kf_ref[0, head, :, 0:QK_NOPE] = knope[:, sl].astype(BF16)
            kf_ref[0, head, :, QK_NOPE:2 * QK_NOPE] = kpe_pad
            qf_ref[0, head, :, 0:QK_NOPE] = (qnope[:, sl] * Q_SCALE).astype(BF16)
            blk = pe_blocks[head // 2]
            if head % 2:
                blk = pltpu.roll(blk, QK_ROPE, axis=1)
            qf_ref[0, head, :, QK_NOPE:2 * QK_NOPE] = blk.astype(BF16)

    cosb = cosb_ref[...]
    sinb = sinb_ref[...]
    rq = _dot_nt(h, wr_ref[0:D, :])
    rk = _dot_nt(h, wr_ref[D:2 * D, :])
    for head in range(RET_HEADS):
        sl = slice(head * RET_DK, (head + 1) * RET_DK)
        rq_ref[0, :, sl] = _rope(rq[:, sl], cosb, sinb, RET_DK // 2).astype(rq_ref.dtype)
        rk_h = _rope(rk[:, sl], cosb, sinb, RET_DK // 2) * (RET_DK ** -0.5)
        if absorb:
            rk_ref[0, :, sl] = rk_h.astype(rk_ref.dtype)
        else:
            rk_ref[0, sl, :] = rk_h.T.astype(rk_ref.dtype)
    rv_ref[0] = _dot_nt(h, wr_ref[2 * D:3 * D, :]).astype(rv_ref.dtype)
    rg = _dot_nt(h, wr_ref[3 * D:4 * D, :])
    ga_ref[0] = jax.nn.sigmoid(_dot_nt(h, wr_ref[4 * D:5 * D, :])).astype(BF16)
    gr = jax.nn.sigmoid(_dot_nt(h, wr_ref[5 * D:6 * D, :]))
    gr_ref[0] = (gr * (rg * jax.nn.sigmoid(rg))).astype(BF16)


def _inproj(x3, mod3, tabs, w, *, tm, act_dtype, absorb):
    G, Sg, D = x3.shape
    R = mod3.shape[1]
    assert Sg % tm == 0 and R in (1, tm) and (R == 1 or Sg == tm)
    row = lambda g, i: (g, i, 0)
    head_row = lambda g, i: (g, 0, i, 0)
    tab = pl.BlockSpec((tm, LANES), lambda g, i: (i, 0))
    w_uk = w["w_ukT"] if absorb else w["w_uk_flat"]
    in_specs = [
        pl.BlockSpec((1, tm, D), row),
        pl.BlockSpec((1, R, 2 * D), lambda g, i: (g, 0, 0)),
        tab, tab, tab, tab,
        _resident(w["w_head"].shape), _resident(w["w_r"].shape),
        _resident(w["g_q"].shape), _resident(w["g_kv"].shape),
        _resident(w["w_uq_nope"].shape), _resident(w["w_uq_pe"].shape), _resident(w_uk.shape),
    ]

    def tok(width, dtype):
        return jax.ShapeDtypeStruct((G, Sg, width), dtype), pl.BlockSpec((1, tm, width), row)

    def per_head(width):
        return (jax.ShapeDtypeStruct((G, MLA_HEADS, Sg, width), BF16),
                pl.BlockSpec((1, MLA_HEADS, tm, width), head_row))

    kpe_t = (jax.ShapeDtypeStruct((G, QK_ROPE, Sg), F32), pl.BlockSpec((1, QK_ROPE, tm), lambda g, i: (g, 0, i)))
    outs = [tok(KV_LORA, F32), kpe_t, tok(KV_LORA, BF16)]
    if absorb:
        outs += [tok(QK_ROPE, BF16), per_head(KV_LORA), per_head(QK_ROPE)]
    else:
        outs += [per_head(2 * QK_NOPE), per_head(2 * QK_NOPE)]
    ret_k = tok(D, act_dtype) if absorb else (jax.ShapeDtypeStruct((G, D, Sg), act_dtype),
                                              pl.BlockSpec((1, D, tm), lambda g, i: (g, 0, i)))
    outs += [tok(D, act_dtype), ret_k, tok(D, act_dtype), tok(D, BF16), tok(D, BF16)]
    return pl.pallas_call(
        functools.partial(_inproj_kernel, absorb=absorb),
        grid=(G, Sg // tm),
        in_specs=in_specs, out_specs=[o[1] for o in outs], out_shape=[o[0] for o in outs],
        compiler_params=_params(("parallel", "parallel")),
        name="inproj",
    )(x3, mod3, *tabs, w["w_head"], w["w_r"], w["g_q"], w["g_kv"],
      w["w_uq_nope"], w["w_uq_pe"], w_uk)


def _mla_prompt_kernel(qf_ref, kf_ref, v_ref, wuv_ref, o_ref,
                       m_ref, l_ref, acc_ref, s_ref, *, tq):
    qi = pl.program_id(1)

    def rows_of(j):
        return pl.ds(pl.multiple_of(j * tq, tq), tq)

    def scores(head, j):
        return _dot_nt(qf_ref[0, head], kf_ref[0, head, rows_of(j), :])

    for hd in range(SCORE_LOOKAHEAD):
        s_ref[hd] = scores(hd, 0)

    def tile(j, first, last):
        v = v_ref[0, rows_of(j), :]
        col = lax.broadcasted_iota(jnp.int32, (tq, tq), 1)
        row = lax.broadcasted_iota(jnp.int32, (tq, tq), 0)
        if first:
            keep = col <= row + qi * tq
        elif last:
            keep = col <= row
        pending = [s_ref[hd] for hd in range(SCORE_LOOKAHEAD)]
        for head in range(MLA_HEADS):
            s = pending.pop(0)
            ahead = head + SCORE_LOOKAHEAD
            if ahead < MLA_HEADS:
                pending.append(scores(ahead, j))
            elif not last:
                s_ref[ahead - MLA_HEADS] = scores(ahead - MLA_HEADS, j + 1)
            if first or last:
                s = jnp.where(keep, s, NEG)
            m_cur = jnp.max(s, axis=-1, keepdims=True)
            m_new = jnp.broadcast_to(m_cur, (tq, LANES)) if first else jnp.maximum(m_ref[head], m_cur)
            p = jnp.exp2(s - jnp.concatenate([m_new] * (tq // LANES), axis=1))
            psum = p[:, 0:LANES]
            for c in range(1, tq // LANES):
                psum = psum + p[:, c * LANES:(c + 1) * LANES]
            pv = _dot(p.astype(BF16), v)
            if first:
                l_ref[head] = psum
                acc_ref[head] = pv
            else:
                a = jnp.exp2(m_ref[head] - m_new)
                l_ref[head] = a * l_ref[head] + psum
                acc_ref[head] = jnp.concatenate([a] * (KV_LORA // LANES), axis=1) * acc_ref[head] + pv
            m_ref[head] = m_new

    n_mid = jnp.maximum(qi - 1, 0)

    def pair(i, carry):
        tile(1 + 2 * i, False, False)
        tile(2 + 2 * i, False, False)
        return carry

    tile(0, True, False)
    lax.fori_loop(0, n_mid >> 1, pair, 0)

    @pl.when((n_mid & 1) == 1)
    def _():
        tile(qi - 1, False, False)

    @pl.when(qi > 0)
    def _():
        tile(qi, False, True)

    heads = range(MLA_HEADS)
    inv = [1.0 / jnp.sum(l_ref[head], axis=-1, keepdims=True) for head in heads]
    lat = [(acc_ref[head] * inv[head]).astype(BF16) for head in heads]
    out = [_dot(lat[head], wuv_ref[head]) for head in heads]
    for head in heads:
        o_ref[0, :, head * V_HEAD:(head + 1) * V_HEAD] = out[head].astype(BF16)


def _mla_prompt(qf, kf, v, w_uv, *, tq):
    B, Hh, S, width = qf.shape
    assert tq % LANES == 0 and S >= 2 * tq
    return pl.pallas_call(
        functools.partial(_mla_prompt_kernel, tq=tq),
        grid=(B, S // tq),
        in_specs=[pl.BlockSpec((1, Hh, tq, width), lambda b, i: (b, 0, i, 0)),
                  pl.BlockSpec((1, Hh, S, width), lambda b, i: (b, 0, 0, 0)),
                  pl.BlockSpec((1, S, KV_LORA), lambda b, i: (b, 0, 0)),
                  _resident(w_uv.shape)],
        out_specs=pl.BlockSpec((1, tq, D_MODEL), lambda b, i: (b, i, 0)),
        out_shape=jax.ShapeDtypeStruct((B, S, D_MODEL), BF16),
        scratch_shapes=[pltpu.VMEM((Hh, tq, LANES), F32), pltpu.VMEM((Hh, tq, LANES), F32),
                        pltpu.VMEM((Hh, tq, KV_LORA), F32), pltpu.VMEM((SCORE_LOOKAHEAD, tq, tq), F32)],
        compiler_params=_params(("parallel", "parallel")),
        name="mla_prompt",
    )(qf, kf, v, w_uv)


def _mla_sample_kernel(pt_ref, q1_ref, q2_ref, cn_ref, kn_ref, ck_hbm, kp_hbm, o_ref,
                       ck_buf, kp_buf, sems, *, pages, groups, tokens):
    b = pl.program_id(0)
    n_b = pl.num_programs(0)
    n_pages = pages * groups

    def page_copies(bb, i):
        slot = lax.rem(bb, 2)
        page = pt_ref[bb, i]
        return (pltpu.make_async_copy(ck_hbm.at[0, page], ck_buf.at[slot, i], sems.at[0, slot]),
                pltpu.make_async_copy(kp_hbm.at[0, page], kp_buf.at[slot, i], sems.at[1, slot]))

    def start_page(bb, i):
        for cp in page_copies(bb, i):
            cp.start()

    @pl.when(b == 0)
    def _():
        for i in range(n_pages):
            start_page(b, i)

    for i in range(n_pages):
        for cp in page_copies(b, i):
            cp.wait()

    def start_next_row(i):
        @pl.when(b + 1 < n_b)
        def _():
            start_page(b + 1, i)

    q1 = q1_ref[0]
    q2 = q2_ref[0]
    rows = q1.shape[0]
    slot = lax.rem(b, 2)

    def scores(g):
        cols = []
        for i in range(g * pages, (g + 1) * pages):
            cols.append(_dot_nt(q1, ck_buf[slot, i].astype(BF16)) + _dot(q2, kp_buf[slot, i].astype(BF16)))
            if i < n_pages // NEXT_ROW_BURST:
                for k in range(NEXT_ROW_BURST):
                    start_next_row(NEXT_ROW_BURST * i + k)
        return cols

    def update(g, cols, m, l, acc):
        mx = cols[0]
        for c in cols[1:]:
            mx = jnp.maximum(mx, c)
        m_new = jnp.maximum(m, jnp.max(mx, axis=-1, keepdims=True))
        a = jnp.exp2(m - m_new)
        ps = [jnp.exp2(c - m_new) for c in cols]
        psum = ps[0]
        for p in ps[1:]:
            psum = psum + p
        pv = None
        for j in range(pages):
            part = _dot(ps[j].astype(BF16), ck_buf[slot, g * pages + j].astype(BF16))
            pv = part if pv is None else pv + part
        acc = jnp.concatenate([a] * (KV_LORA // LANES), axis=1) * acc + pv
        return m_new, a * l + psum, acc

    m = jnp.full((rows, LANES), -jnp.inf, F32)
    l = jnp.zeros((rows, LANES), F32)
    acc = jnp.zeros((rows, KV_LORA), F32)
    pending = scores(0)
    for g in range(groups):
        nxt = scores(g + 1) if g + 1 < groups else None
        m, l, acc = update(g, pending, m, l, acc)
        pending = nxt

    q1f = q1.astype(F32)
    q2f = q2.astype(F32)
    cn = cn_ref[0]
    kn = kn_ref[0]
    tok = lax.broadcasted_iota(jnp.int32, (rows, LANES), 0) & (tokens - 1)
    cols = []
    for t in range(tokens):
        sc = (jnp.sum(q1f * cn[t:t + 1, :], axis=-1, keepdims=True)
              + jnp.sum(q2f * kn[t:t + 1, :], axis=-1, keepdims=True))
        cols.append(jnp.where(tok >= t, sc, NEG))
    m_fin = m
    for sc in cols:
        m_fin = jnp.maximum(m_fin, sc)
    a2 = jnp.exp2(m - m_fin)
    l_fin = jnp.sum(a2 * l, axis=-1, keepdims=True)
    acc = jnp.concatenate([a2] * (KV_LORA // LANES), axis=1) * acc
    for t, sc in enumerate(cols):
        pt = jnp.exp2(sc - m_fin)
        l_fin = l_fin + pt[:, 0:1]
        acc = acc + jnp.concatenate([pt] * (KV_LORA // LANES), axis=1) * cn[t:t + 1, :]
    o_ref[0] = acc * (1.0 / l_fin)


def _mla_sample(page_table, q1, q2, cnew, knew, cache_ckv, cache_kpe_t, *, pages):
    DB, n_pages = page_table.shape
    rows = q1.shape[1]
    tokens = cnew.shape[1]
    assert n_pages % pages == 0 and tokens & (tokens - 1) == 0
    groups = n_pages // pages

    per_batch = lambda b, pt: (b, 0, 0)
    grid_spec = pltpu.PrefetchScalarGridSpec(
        num_scalar_prefetch=1,
        grid=(DB,),
        in_specs=[pl.BlockSpec((1, rows, KV_LORA), per_batch),
                  pl.BlockSpec((1, rows, QK_ROPE), per_batch),
                  pl.BlockSpec((1, tokens, KV_LORA), per_batch),
                  pl.BlockSpec((1, tokens, QK_ROPE), per_batch),
                  pl.BlockSpec(memory_space=pl.ANY),
                  pl.BlockSpec(memory_space=pl.ANY)],
        out_specs=pl.BlockSpec((1, rows, KV_LORA), per_batch),
        scratch_shapes=[pltpu.VMEM((2, n_pages, PAGE_SIZE, KV_LORA), F32),
                        pltpu.VMEM((2, n_pages, QK_ROPE, PAGE_SIZE), F32),
                        pltpu.SemaphoreType.DMA((2, 2))],
    )
    return pl.pallas_call(
        functools.partial(_mla_sample_kernel, pages=pages, groups=groups, tokens=tokens),
        grid_spec=grid_spec,
        out_shape=jax.ShapeDtypeStruct((DB, rows, KV_LORA), F32),
        compiler_params=_params(("arbitrary",)),
        name="mla_sample",
    )(page_table, q1, q2, cnew, knew, cache_ckv, cache_kpe_t)


def _retention_kernel(*refs, T, L, has_s0):
    C = RET_CHUNK
    padded = T < C
    if has_s0:
        q_ref, k_ref, v_ref, s0_ref, y_ref, s_ref = refs[:6]
        rest = refs[6:]
    else:
        q_ref, k_ref, v_ref, y_ref, s_ref = refs[:5]
        rest = refs[5:]
    heads = range(RET_HEADS)
    ri = lax.broadcasted_iota(jnp.int32, (C, C), 0).astype(F32)
    ci = lax.broadcasted_iota(jnp.int32, (C, C), 1).astype(F32)
    diff = ri - ci
    causal = diff >= 0.0
    log_gamma = [math.log(1.0 - 2.0 ** (-5.0 - h)) for h in heads]
    dmask = [jnp.where(causal, jnp.exp(jnp.where(causal, diff, 0.0) * lg), 0.0) for lg in log_gamma]
    q_decay = [jnp.exp((ri + 1.0) * lg) for lg in log_gamma]
    k_decay = [jnp.exp((L - 1.0 - ci) * lg) for lg in log_gamma]
    chunk_decay = [math.exp(L * lg) for lg in log_gamma]

    def col(h):
        return slice(h * RET_DK, (h + 1) * RET_DK)

    def chunks(streams):
        pre = []
        for r, items in streams:
            scores = [[_dot(q[h], kt[h].astype(BF16)) for h in heads] for q, kt, v, _ in items]
            kd_t = [[(kt[h].astype(F32) * k_decay[h]).astype(BF16) for h in heads] for q, kt, v, _ in items]
            pre.append((scores, kd_t))
        inner = [[[_dot((sc[h] * dmask[h]).astype(BF16), v[h]) for h in heads]
                  for sc, (q, kt, v, _) in zip(scores, items)]
                 for (scores, _), (r, items) in zip(pre, streams)]
        for (r, items), (_, kd_t), inn_r in zip(streams, pre, inner):
            s = [s_ref[r, h] for h in heads]
            for (q, kt, v, store), inn, kd in zip(items, inn_r, kd_t):
                cross = [_dot(q[h], s[h].astype(BF16)) for h in heads]
                upd = [_dot(kd[h], v[h]) for h in heads]
                for h in heads:
                    store(h, inn[h] + cross[h] * q_decay[h])
                s = [chunk_decay[h] * s[h] + upd[h] for h in heads]
            for h in heads:
                s_ref[r, h] = s[h]

    n_rows = q_ref.shape[0]
    for r in range(n_rows):
        for h in heads:
            s_ref[r, h] = s0_ref[r, h] if has_s0 else jnp.zeros((RET_DK, RET_DV), F32)
    if padded:
        qp_ref, kp_ref, vp_ref = rest
        streams = []
        for r in range(n_rows):
            for src, dst in zip((q_ref, k_ref, v_ref), rest):
                dst[r] = jnp.zeros(dst.shape[1:], F32)
                dst[r, 0:T, :] = src[r].astype(F32)

            def store(h, y, r=r):
                y_ref[r, :, col(h)] = y[0:T, :].astype(y_ref.dtype)
            streams.append((r, [([qp_ref[r, :, col(h)].astype(BF16) for h in heads],
                                 [kp_ref[r, :, col(h)].T for h in heads],
                                 [vp_ref[r, :, col(h)].astype(BF16) for h in heads], store)]))
        chunks(streams)
    else:
        per_trip = RET_CHUNKS_PER_TRIP if (T // C) % RET_CHUNKS_PER_TRIP == 0 else 1

        def item(c):
            rows = pl.ds(pl.multiple_of(c * C, C), C)

            def store(h, y):
                y_ref[0, rows, col(h)] = y.astype(y_ref.dtype)
            return ([q_ref[0, rows, col(h)] for h in heads],
                    [k_ref[0, col(h), rows] for h in heads],
                    [v_ref[0, rows, col(h)] for h in heads], store)

        def body(i, carry):
            chunks([(0, [item(i * per_trip + k) for k in range(per_trip)])])
            return carry
        lax.fori_loop(0, T // C // per_trip, body, 0)


def _retention(rq, rk, rv, s0, *, L, out_dtype):
    B, T, D = rq.shape
    has_s0 = s0 is not None
    padded = T < RET_CHUNK
    assert padded or T % RET_CHUNK == 0
    rb = RET_ROWS_PER_STEP if padded and B % RET_ROWS_PER_STEP == 0 else 1
    tok = pl.BlockSpec((rb, T, D), lambda b: (b, 0, 0))
    key = tok if padded else pl.BlockSpec((rb, D, T), lambda b: (b, 0, 0))
    state = pl.BlockSpec((rb, RET_HEADS, RET_DK, RET_DV), lambda b: (b, 0, 0, 0))
    in_specs = [tok, key, tok] + ([state] if has_s0 else [])
    args = (rq, rk, rv) + ((s0,) if has_s0 else ())
    return pl.pallas_call(
        functools.partial(_retention_kernel, T=T, L=L, has_s0=has_s0),
        grid=(B // rb,),
        in_specs=in_specs,
        out_specs=[tok, state],
        out_shape=[jax.ShapeDtypeStruct((B, T, D), out_dtype),
                   jax.ShapeDtypeStruct((B, RET_HEADS, RET_DK, RET_DV), F32)],
        scratch_shapes=[pltpu.VMEM((rb, RET_CHUNK, D), F32)] * 3 if padded else [],
        compiler_params=_params(("parallel",)),
        name="retention",
    )(*args)


def _post_kernel(*refs, apply_uv):
    if apply_uv:
        (x_ref, att_ref, y_ref, ga_ref, gr_ref, mod_ref, wuv_ref, wo_ref, wup_ref, wdn_ref,
         g1_ref, b1_ref, g2_ref, b2_ref, o_ref) = refs
    else:
        (x_ref, att_ref, y_ref, ga_ref, gr_ref, mod_ref, wo_ref, wup_ref, wdn_ref,
         g1_ref, b1_ref, g2_ref, b2_ref, o_ref) = refs
    D = D_MODEL
    tm = x_ref.shape[1]
    n_sub = POST_SPLIT if tm % (POST_SPLIT * 16) == 0 else 1
    subs = [slice(i * (tm // n_sub), (i + 1) * (tm // n_sub)) for i in range(n_sub)]

    def mod(rows, k):
        cols = slice(k * D, (k + 1) * D)
        return mod_ref[0, :, cols] if mod_ref.shape[1] == 1 else mod_ref[0, rows, cols]

    def mix_in(rows):
        if apply_uv:
            o_a = jnp.concatenate([_dot(att_ref[0, head, rows, :].astype(BF16), wuv_ref[head])
                                   for head in range(MLA_HEADS)], axis=1).astype(BF16)
        else:
            o_a = att_ref[0, rows, :]
        y = y_ref[0, rows, :].astype(F32)
        normed = []
        for head in range(RET_HEADS):
            yh = y[:, head * RET_DV:(head + 1) * RET_DV]
            yc = yh - jnp.mean(yh, axis=-1, keepdims=True)
            normed.append((yc * lax.rsqrt(jnp.mean(yc * yc, axis=-1, keepdims=True) + LN_EPS)).astype(BF16))
        return ga_ref[0, rows, :] * o_a + gr_ref[0, rows, :] * jnp.concatenate(normed, axis=1)

    mixv = [mix_in(rows) for rows in subs]
    mix = [_dot(mv, wo_ref[...]) for mv in mixv]
    x1 = [_layer_norm(ALPHA * x_ref[0, rows, :] + (1.0 + mod(rows, 2)) * mx, g1_ref[...], b1_ref[...])
          for rows, mx in zip(subs, mix)]
    h2 = [(x * (1.0 + mod(rows, 4)) + mod(rows, 3)).astype(BF16) for rows, x in zip(subs, x1)]

    def up(unit):
        i, c = unit
        return _dot(h2[i], wup_ref[:, c * D:(c + 1) * D])

    units = [(i, c) for i in range(n_sub) for c in range(D_FF // D)]
    m = [None] * n_sub
    u_next = up(units[0])
    for idx, (i, c) in enumerate(units):
        u = jnp.maximum(u_next, 0.0)
        if idx + 1 < len(units):
            u_next = up(units[idx + 1])
        part = _dot((u * u).astype(BF16), wdn_ref[c * D:(c + 1) * D, :])
        m[i] = part if m[i] is None else m[i] + part
    for i, rows in enumerate(subs):
        o_ref[0, rows, :] = _layer_norm(ALPHA * x1[i] + (1.0 + mod(rows, 5)) * m[i], g2_ref[...], b2_ref[...])


def _post(x3, att, y3, ga, gr, mod3, w, *, tm, apply_uv):
    G, Sg, D = x3.shape
    R = mod3.shape[1]
    row = lambda g, i: (g, i, 0)
    act = pl.BlockSpec((1, tm, D), row)
    if apply_uv:
        att_spec = pl.BlockSpec((1, MLA_HEADS, tm, KV_LORA), lambda g, i: (g, 0, i, 0))
    else:
        att_spec = act
    in_specs = [act, att_spec, act, act, act, pl.BlockSpec((1, R, 6 * D), lambda g, i: (g, 0, 0))]
    args = [x3, att, y3, ga, gr, mod3]
    names = (["w_uv"] if apply_uv else []) + ["w_o", "w_up", "w_down", "ln1_g", "ln1_b", "ln2_g", "ln2_b"]
    for n in names:
        in_specs.append(_resident(w[n].shape))
        args.append(w[n])
    return pl.pallas_call(
        functools.partial(_post_kernel, apply_uv=apply_uv),
        grid=(G, Sg // tm),
        in_specs=in_specs,
        out_specs=act,
        out_shape=jax.ShapeDtypeStruct((G, Sg, D), F32),
        compiler_params=_params(("parallel", "parallel")),
        name="post",
    )(*args)


def _rope_tables(pos, dim):
    half = dim // 2
    inv = ROPE_BASE ** (-jnp.arange(half, dtype=F32) / half)
    ang = pos.astype(F32)[:, None] * inv[None, :]
    cos, sin = jnp.cos(ang), jnp.sin(ang)
    reps = LANES // dim
    return (jnp.tile(jnp.concatenate([cos, cos], -1), (1, reps)),
            jnp.tile(jnp.concatenate([-sin, sin], -1), (1, reps)))


def kernel(x_prompt, x_sample, c_prompt, c_sample, cache_ckv, cache_kpe, state_ret, page_table,
           w_ada, b_ada, w_in, g_qnorm, g_kvnorm, w_uq, w_uk, w_uv, w_o,
           ln1_g, ln1_b, w_up, w_down, ln2_g, ln2_b):
    B, S, D = x_prompt.shape
    DB, T, _ = x_sample.shape
    n_pages = page_table.shape[1]
    past_len = n_pages * PAGE_SIZE
    assert DEPTH == 1 and w_in.shape[0] == 1
    l = 0
    n_head = Q_LORA + KV_LORA + QK_ROPE

    w_in_t = jnp.swapaxes(w_in[l], 0, 1)
    w = {
        "w_head": jnp.concatenate([w_in_t[:n_head], w_in_t[Q_LORA + KV_LORA:n_head]], axis=0).astype(BF16),
        "w_r": w_in_t[n_head:].astype(BF16),
        "g_q": g_qnorm[l][None, :], "g_kv": g_kvnorm[l][None, :],
        "w_uq_nope": w_uq[l][:, :, :QK_NOPE].reshape(Q_LORA, MLA_HEADS * QK_NOPE).astype(BF16),
        "w_uq_pe": w_uq[l][:, :, QK_NOPE:].reshape(Q_LORA, MLA_HEADS * QK_ROPE).astype(BF16),
        "w_ukT": jnp.transpose(w_uk[l], (1, 2, 0)).astype(BF16),
        "w_uk_flat": w_uk[l].reshape(KV_LORA, MLA_HEADS * QK_NOPE).astype(BF16),
        "w_uv": jnp.transpose(w_uv[l], (1, 0, 2)).astype(BF16),
        "w_o": w_o[l].astype(BF16), "w_up": w_up[l].astype(BF16), "w_down": w_down[l].astype(BF16),
        "ln1_g": ln1_g[l][None, :], "ln1_b": ln1_b[l][None, :],
        "ln2_g": ln2_g[l][None, :], "ln2_b": ln2_b[l][None, :],
    }

    n_c = B + DB
    pad = (-n_c) % 16
    c_all = jnp.concatenate([c_prompt, c_sample, jnp.zeros((pad, D), F32)], axis=0)
    ada = _ada(c_all, w_ada[l], b_ada[l][None, :])
    mod_p = ada[:B].reshape(B, 1, 6 * D)
    mod_s = jnp.repeat(ada[B:n_c], T, axis=0).reshape(1, DB * T, 6 * D)

    pos_p = jnp.arange(S)
    tabs_p = _rope_tables(pos_p, QK_ROPE) + _rope_tables(pos_p, RET_DK)
    (ckv_p, kpe_p, kc_p, qf_p, kf_p, rq_p, rk_p, rv_p, ga_p, gr_p) = _inproj(
        x_prompt, mod_p, tabs_p, w, tm=PROMPT_ROW_TILE, act_dtype=BF16, absorb=False)
    kpe_p = jnp.swapaxes(kpe_p, 1, 2)
    oa_p = _mla_prompt(qf_p, kf_p, kc_p, w["w_uv"], tq=ATTN_TILE)
    y_p, ret_p = _retention(rq_p, rk_p, rv_p, None, L=min(RET_CHUNK, S), out_dtype=BF16)
    out_p = _post(x_prompt, oa_p, y_p, ga_p, gr_p, mod_p, w, tm=PROMPT_ROW_TILE, apply_uv=False)

    n_s = DB * T
    pos_s = jnp.tile(past_len + jnp.arange(T), DB)
    tabs_s = _rope_tables(pos_s, QK_ROPE) + _rope_tables(pos_s, RET_DK)
    xs = x_sample.reshape(1, n_s, D)
    (ckv_s, kpe_s, _, _, qlat_s, qpe_s, rq_s, rk_s, rv_s, ga_s, gr_s) = _inproj(
        xs, mod_s, tabs_s, w, tm=n_s, act_dtype=F32, absorb=True)
    kpe_s = jnp.swapaxes(kpe_s, 1, 2)

    def to_batch_rows(q):
        d = q.shape[-1]
        return q[0].reshape(MLA_HEADS, DB, T, d).transpose(1, 0, 2, 3).reshape(DB, MLA_HEADS * T, d)

    olat_s = _mla_sample(page_table, to_batch_rows(qlat_s), to_batch_rows(qpe_s),
                         ckv_s.reshape(DB, T, KV_LORA), kpe_s.reshape(DB, T, QK_ROPE),
                         cache_ckv, jnp.swapaxes(cache_kpe, 2, 3), pages=SAMPLE_PAGE_GROUP)
    olat_s = olat_s.reshape(DB, MLA_HEADS, T, KV_LORA).transpose(1, 0, 2, 3).reshape(1, MLA_HEADS, n_s, KV_LORA)
    y_s, ret_s = _retention(rq_s.reshape(DB, T, D), rk_s.reshape(DB, T, D), rv_s.reshape(DB, T, D),
                            state_ret[l], L=T, out_dtype=F32)
    out_s = _post(xs, olat_s, y_s.reshape(1, n_s, D), ga_s, gr_s, mod_s, w, tm=n_s, apply_uv=True)

    return (out_p, out_s.reshape(DB, T, D),
            ckv_p[None], kpe_p[None], ret_p[None],
            ckv_s.reshape(1, DB, T, KV_LORA), kpe_s.reshape(1, DB, T, QK_ROPE), ret_s[None])
```
